```python
import jax, jax.numpy as jnp
from jax import lax
import numpy as np


D_MODEL = 1024
BATCH = 16
SEQ = 2048
DEPTH = 1

MEM_LEN = 256

GMLP_WIDTH = 1024
GMLP_GROUPS = 8
GMLP_CHUNK = 128
LRU_WIDTH = 1024
LRU_BLOCKS = 8
LRU_BLOCK_DIM = LRU_WIDTH // LRU_BLOCKS
CONV_WIDTH = 4
LRU_C = 8.0
XA_HEADS = 4
XA_HEAD_DIM = 256
XA_WIDTH = XA_HEADS * XA_HEAD_DIM
N_BRANCHES = 3
IN_COLS = 2 * GMLP_WIDTH + 2 * LRU_WIDTH + XA_WIDTH + N_BRANCHES * D_MODEL
N_EXPERTS = 64
TOP_K = 8
N_GROUPS = 8
TOPK_GROUPS = 4
EXPERT_DIM = 256
SHARED_DIM = 256
ROUTED_SCALE = 2.5
EXPERT_BLOCK = 256
LN_EPS = 1e-5
DN_ALPHA = (2.0 * DEPTH) ** 0.25
DN_BETA = (8.0 * DEPTH) ** -0.25

kernel_name = "hybrid_gmlp_rglru_xattn_moe_block"


def layer_norm(x, g, b):
    xf = x.astype(jnp.float32)
    mu = jnp.mean(xf, axis=-1, keepdims=True)
    var = jnp.mean(jnp.square(xf - mu), axis=-1, keepdims=True)
    return ((xf - mu) * lax.rsqrt(var + LN_EPS)).astype(x.dtype) * g + b


def chunked_spatial_gating(uv, norm_g, norm_b, w_sp, b_sp):
    u, v = jnp.split(jax.nn.gelu(uv), 2, axis=-1)
    v = layer_norm(v, norm_g, norm_b)
    B, S, _ = v.shape
    n_chunks = S // GMLP_CHUNK
    vc = v.reshape(B, n_chunks, GMLP_CHUNK, GMLP_GROUPS, GMLP_WIDTH // GMLP_GROUPS)
    causal = jnp.tril(jnp.ones((GMLP_CHUNK, GMLP_CHUNK), dtype=bool))
    w = jnp.where(causal[None], w_sp, 0)
    mixed = jnp.einsum('gts,bnsgc->bntgc', w, vc) + b_sp.T[None, None, :, :, None]
    return u * mixed.reshape(B, S, GMLP_WIDTH)


def causal_depthwise_conv(x, w, b):
    S = x.shape[1]
    xp = jnp.pad(x, ((0, 0), (CONV_WIDTH - 1, 0), (0, 0)))
    out = b
    for k in range(CONV_WIDTH):
        out = out + w[k] * xp[:, CONV_WIDTH - 1 - k: CONV_WIDTH - 1 - k + S]
    return out


def block_diag_linear(x, w, b):
    B, S, _ = x.shape
    xg = x.reshape(B, S, LRU_BLOCKS, LRU_BLOCK_DIM)
    return jnp.einsum('bsgi,gio->bsgo', xg, w).reshape(B, S, LRU_WIDTH) + b


def rg_lru(x, w_a, b_a, w_i, b_i, lam):
    r = jax.nn.sigmoid(block_diag_linear(x, w_a, b_a))
    i = jax.nn.sigmoid(block_diag_linear(x, w_i, b_i))
    log_a = -LRU_C * r.astype(jnp.float32) * jax.nn.softplus(-lam.astype(jnp.float32))
    a = jnp.exp(log_a)
    mult = jnp.sqrt(-jnp.expm1(2.0 * log_a))
    b_term = mult * (i * x).astype(jnp.float32)

    def combine(c1, c2):
        a1, b1 = c1
        a2, b2 = c2
        return a1 * a2, a2 * b1 + b2

    _, h = lax.associative_scan(combine, (a, b_term), axis=1)
    return h.astype(x.dtype)


def memory_cross_attention(q, mem, w_kv):
    B, S, _ = q.shape
    M = mem.shape[1]
    k, v = jnp.split(mem @ w_kv, 2, axis=-1)
    q = q.reshape(B, S, XA_HEADS, XA_HEAD_DIM)
    k = k.reshape(B, M, XA_HEADS, XA_HEAD_DIM)
    v = v.reshape(B, M, XA_HEADS, XA_HEAD_DIM)
    scores = jnp.einsum('bshd,bmhd->bhsm', q, k).astype(jnp.float32) * (XA_HEAD_DIM ** -0.5)
    p = jax.nn.softmax(scores, axis=-1).astype(v.dtype)
    return jnp.einsum('bhsm,bmhd->bshd', p, v).reshape(B, S, XA_WIDTH)


def route(h, w_router, router_bias):
    T = h.shape[0]
    s = jax.nn.sigmoid((h @ w_router).astype(jnp.float32))
    choice = s + router_bias.astype(jnp.float32)
    grp = choice.reshape(T, N_GROUPS, N_EXPERTS // N_GROUPS)
    grp_score = jnp.sum(lax.top_k(grp, 2)[0], axis=-1)
    _, top_grp = lax.top_k(grp_score, TOPK_GROUPS)
    grp_mask = jnp.any(top_grp[:, :, None] == jnp.arange(N_GROUPS)[None, None, :], axis=1)
    exp_mask = jnp.repeat(grp_mask, N_EXPERTS // N_GROUPS, axis=-1)
    _, idx = lax.top_k(jnp.where(exp_mask, choice, -jnp.inf), TOP_K)
    w = jnp.take_along_axis(s, idx, axis=-1)
    w = w / jnp.sum(w, axis=-1, keepdims=True) * ROUTED_SCALE
    return idx, w


def routed_experts(h, idx, wts, w_gate, w_up, w_down):
    T, D = h.shape
    A = T * TOP_K
    n_blocks = -(-A // EXPERT_BLOCK) + N_EXPERTS
    P = n_blocks * EXPERT_BLOCK
    flat_e = idx.reshape(A)
    flat_tok = jnp.arange(A, dtype=jnp.int32) // TOP_K
    flat_w = wts.reshape(A)
    order = jnp.argsort(flat_e)
    sorted_e = flat_e[order]
    counts = jnp.bincount(flat_e, length=N_EXPERTS)
    padded = (counts + EXPERT_BLOCK - 1) // EXPERT_BLOCK * EXPERT_BLOCK
    pad_end = jnp.cumsum(padded)
    pad_start = pad_end - padded
    start = jnp.cumsum(counts) - counts
    dest = pad_start[sorted_e] + (jnp.arange(A, dtype=jnp.int32) - start[sorted_e])
    row_tok = jnp.zeros((P,), jnp.int32).at[dest].set(flat_tok[order])
    row_w = jnp.zeros((P,), wts.dtype).at[dest].set(flat_w[order])
    block_start = jnp.arange(n_blocks, dtype=jnp.int32) * EXPERT_BLOCK
    block_e = jnp.minimum(jnp.searchsorted(pad_end, block_start, side='right'), N_EXPERTS - 1)

    def body(acc, blk):
        tok, rw, e = blk
        xb = h[tok]
        yb = (jax.nn.silu(xb @ w_gate[e]) * (xb @ w_up[e])) @ w_down[e]
        return acc.at[tok].add(yb * rw[:, None].astype(yb.dtype)), None

    acc, _ = lax.scan(body, jnp.zeros_like(h),
                      (row_tok.reshape(n_blocks, EXPERT_BLOCK),
                       row_w.reshape(n_blocks, EXPERT_BLOCK), block_e))
    return acc


def hybrid_layer(h, mem, w_in, gmlp_norm_g, gmlp_norm_b, gmlp_spatial_w, gmlp_spatial_b,
                 conv_w, conv_b, lru_wa, lru_ba, lru_wi, lru_bi, lru_lambda, w_kv,
                 w_br_gmlp, w_br_lru, w_br_xa, w_out, ln1_g, ln1_b,
                 w_router, router_bias, exp_gate, exp_up, exp_down, sh_gate, sh_up, sh_down, ln2_g, ln2_b):
    B, S, D = h.shape
    z = h @ w_in
    c1 = 2 * GMLP_WIDTH
    c2 = c1 + LRU_WIDTH
    c3 = c2 + LRU_WIDTH
    c4 = c3 + XA_WIDTH
    uv, lru_x, lru_g, q, gates = jnp.split(z, [c1, c2, c3, c4], axis=-1)
    y_gmlp = chunked_spatial_gating(uv, gmlp_norm_g, gmlp_norm_b, gmlp_spatial_w, gmlp_spatial_b)
    xc = causal_depthwise_conv(lru_x, conv_w, conv_b)
    y_lru = rg_lru(xc, lru_wa, lru_ba, lru_wi, lru_bi, lru_lambda) * jax.nn.gelu(lru_g)
    y_xa = memory_cross_attention(q, mem, w_kv)
    g = jax.nn.sigmoid(gates).reshape(B, S, N_BRANCHES, D)
    merged = (g[:, :, 0] * (y_gmlp @ w_br_gmlp)
              + g[:, :, 1] * (y_lru @ w_br_lru)
              + g[:, :, 2] * (y_xa @ w_br_xa))
    h = layer_norm(DN_ALPHA * h + merged @ w_out, ln1_g, ln1_b)
    hf = h.reshape(B * S, D)
    idx, wts = route(hf, w_router, router_bias)
    routed = routed_experts(hf, idx, wts, exp_gate, exp_up, exp_down)
    shared = (jax.nn.silu(hf @ sh_gate) * (hf @ sh_up)) @ sh_down
    return layer_norm(DN_ALPHA * h + (routed + shared).reshape(B, S, D), ln2_g, ln2_b)


def setup_inputs(seed: int = 0) -> dict:
    key = jax.random.key(seed)
    ks = iter(jax.random.split(key, 48))
    f32 = jnp.float32
    L, D = DEPTH, D_MODEL

    def nrm(shape, scale):
        return jax.random.normal(next(ks), shape, f32) * scale

    u = jax.random.uniform(next(ks), (L, LRU_WIDTH), f32, 0.9, 0.999)
    p = u ** (1.0 / LRU_C)
    lam = jnp.log(p) - jnp.log1p(-p)
    w_kv = jnp.concatenate([nrm((L, D, XA_WIDTH), D ** -0.5),
                            nrm((L, D, XA_WIDTH), DN_BETA * D ** -0.5)], axis=-1)
    return {
        'x': nrm((BATCH, SEQ, D), 1.0),
        'mem': nrm((BATCH, MEM_LEN, D), 1.0),
        'ln_in_g': 1.0 + nrm((D,), 0.02),
        'ln_in_b': nrm((D,), 0.02),
        'w_in': nrm((L, D, IN_COLS), D ** -0.5),
        'gmlp_norm_g': 1.0 + nrm((L, GMLP_WIDTH), 0.02),
        'gmlp_norm_b': nrm((L, GMLP_WIDTH), 0.02),
        'gmlp_spatial_w': nrm((L, GMLP_GROUPS, GMLP_CHUNK, GMLP_CHUNK), GMLP_CHUNK ** -0.5),
        'gmlp_spatial_b': 1.0 + nrm((L, GMLP_GROUPS, GMLP_CHUNK), 0.1),
        'conv_w': nrm((L, CONV_WIDTH, LRU_WIDTH), CONV_WIDTH ** -0.5),
        'conv_b': nrm((L, LRU_WIDTH), 0.02),
        'lru_wa': nrm((L, LRU_BLOCKS, LRU_BLOCK_DIM, LRU_BLOCK_DIM), LRU_BLOCK_DIM ** -0.5),
        'lru_ba': nrm((L, LRU_WIDTH), 0.02),
        'lru_wi': nrm((L, LRU_BLOCKS, LRU_BLOCK_DIM, LRU_BLOCK_DIM), LRU_BLOCK_DIM ** -0.5),
        'lru_bi': nrm((L, LRU_WIDTH), 0.02),
        'lru_lambda': lam,
        'w_kv': w_kv,
        'w_br_gmlp': nrm((L, GMLP_WIDTH, D), DN_BETA * GMLP_WIDTH ** -0.5),
        'w_br_lru': nrm((L, LRU_WIDTH, D), DN_BETA * LRU_WIDTH ** -0.5),
        'w_br_xa': nrm((L, XA_WIDTH, D), DN_BETA * XA_WIDTH ** -0.5),
        'w_out': nrm((L, D, D), DN_BETA * D ** -0.5),
        'ln1_g': 1.0 + nrm((L, D), 0.02),
        'ln1_b': nrm((L, D), 0.02),
        'w_router': nrm((L, D, N_EXPERTS), D ** -0.5),
        'router_bias': nrm((L, N_EXPERTS), 0.01),
        'exp_gate': nrm((L, N_EXPERTS, D, EXPERT_DIM), DN_BETA * D ** -0.5),
        'exp_up': nrm((L, N_EXPERTS, D, EXPERT_DIM), DN_BETA * D ** -0.5),
        'exp_down': nrm((L, N_EXPERTS, EXPERT_DIM, D), DN_BETA * EXPERT_DIM ** -0.5),
        'sh_gate': nrm((L, D, SHARED_DIM), DN_BETA * D ** -0.5),
        'sh_up': nrm((L, D, SHARED_DIM), DN_BETA * D ** -0.5),
        'sh_down': nrm((L, SHARED_DIM, D), DN_BETA * SHARED_DIM ** -0.5),
        'ln2_g': 1.0 + nrm((L, D), 0.02),
        'ln2_b': nrm((L, D), 0.02),
    }


def reference(x, mem, ln_in_g, ln_in_b, w_in, gmlp_norm_g, gmlp_norm_b, gmlp_spatial_w, gmlp_spatial_b,
              conv_w, conv_b, lru_wa, lru_ba, lru_wi, lru_bi, lru_lambda, w_kv,
              w_br_gmlp, w_br_lru, w_br_xa, w_out, ln1_g, ln1_b,
              w_router, router_bias, exp_gate, exp_up, exp_down, sh_gate, sh_up, sh_down, ln2_g, ln2_b):
    h = layer_norm(x, ln_in_g, ln_in_b)
    for l in range(DEPTH):
        h = hybrid_layer(h, mem, w_in[l], gmlp_norm_g[l], gmlp_norm_b[l], gmlp_spatial_w[l], gmlp_spatial_b[l],
                         conv_w[l], conv_b[l], lru_wa[l], lru_ba[l], lru_wi[l], lru_bi[l], lru_lambda[l], w_kv[l],
                         w_br_gmlp[l], w_br_lru[l], w_br_xa[l], w_out[l], ln1_g[l], ln1_b[l],
                         w_router[l], router_bias[l], exp_gate[l], exp_up[l], exp_down[l],
                         sh_gate[l], sh_up[l], sh_down[l], ln2_g[l], ln2_b[l])
    return h
```

```python
import functools

import jax
import jax.numpy as jnp
from jax import lax
from jax.experimental import pallas as pl
from jax.experimental.pallas import tpu as pltpu

F32 = jnp.float32
BF16 = jnp.bfloat16

GMLP_GROUPS = 8
GMLP_CHUNK = 128
LRU_BLOCKS = 8
CONV_WIDTH = 4
LRU_C = 8.0
XA_HEADS = 4
N_EXPERTS = 64
TOP_K = 8
N_GROUPS = 8
TOPK_GROUPS = 4
ROUTED_SCALE = 2.5
EXPERT_BLOCK = 256
LN_EPS = 1e-5
DEPTH = 1
DN_ALPHA = (2.0 * DEPTH) ** 0.25

VMEM_LIMIT_V7X = 56 * 1024 * 1024


def _cparams(sem):
    return pltpu.CompilerParams(dimension_semantics=sem, vmem_limit_bytes=VMEM_LIMIT_V7X)


def _ln(x, g, b):
    mu = jnp.mean(x, axis=-1, keepdims=True)
    xc = x - mu
    var = jnp.mean(xc * xc, axis=-1, keepdims=True)
    return xc * lax.rsqrt(var + LN_EPS) * g + b


def _gelu(x):
    return x * (0.5 * (1.0 + jnp.tanh(0.7978845608028654 * (x + 0.044715 * (x * x * x)))))


def _sigmoid(x):
    return 0.5 * (1.0 + jnp.tanh(0.5 * x))


def _dot(a, b):
    return jnp.dot(a, b, preferred_element_type=F32)


def _inproj_kernel(x_ref, lg_ref, lb_ref, w_ref, ng_ref, nb_ref,
                   u_ref, v_ref, lx_ref, gl_ref, q_ref, g0_ref, g1_ref, g2_ref, *, d, q_scale):
    hb = _ln(x_ref[...], lg_ref[...], lb_ref[...]).astype(BF16)

    def z(j):
        return _dot(hb, w_ref[:, j * d:(j + 1) * d])

    u_ref[...] = _gelu(z(0)).astype(BF16)
    v_ref[...] = _ln(_gelu(z(1)), ng_ref[...], nb_ref[...]).astype(BF16)
    lx_ref[...] = z(2).astype(BF16)
    gl_ref[...] = _gelu(z(3)).astype(BF16)
    q_ref[...] = (z(4) * q_scale).astype(BF16)
    g0_ref[...] = _sigmoid(z(5)).astype(BF16)
    g1_ref[...] = _sigmoid(z(6)).astype(BF16)
    g2_ref[...] = _sigmoid(z(7)).astype(BF16)


def _inproj(x2, ln_g, ln_b, w_in, ng, nb, tm, q_scale):
    t, d = x2.shape
    n_out = 8
    row = pl.BlockSpec((tm, d), lambda i: (i, 0))
    vec = pl.BlockSpec((1, d), lambda i: (0, 0))
    wspec = pl.BlockSpec(w_in.shape, lambda i: (0, 0), pipeline_mode=pl.Buffered(1))
    return pl.pallas_call(
        functools.partial(_inproj_kernel, d=d, q_scale=q_scale),
        grid=(t // tm,),
        in_specs=[row, vec, vec, wspec, vec, vec],
        out_specs=[row] * n_out,
        out_shape=[jax.ShapeDtypeStruct((t, d), BF16)] * n_out,
        compiler_params=_cparams(("parallel",)),
        name="inproj",
    )(x2, ln_g, ln_b, w_in, ng, nb)


def _kv_kernel(mem_ref, wkt_ref, wv_ref, kt_ref, v_ref):
    m = mem_ref[0].astype(BF16)
    kt = lax.dot_general(wkt_ref[...], m, (((1,), (1,)), ((), ())), preferred_element_type=F32)
    kt_ref[0] = kt.astype(BF16)
    v_ref[0] = _dot(m, wv_ref[...]).astype(BF16)


def _kv(mem, wkt, wv):
    b, m, d = mem.shape
    xw = wv.shape[1]
    return pl.pallas_call(
        _kv_kernel,
        grid=(b,),
        in_specs=[pl.BlockSpec((1, m, d), lambda i: (i, 0, 0)),
                  pl.BlockSpec(wkt.shape, lambda i: (0, 0)),
                  pl.BlockSpec(wv.shape, lambda i: (0, 0))],
        out_specs=[pl.BlockSpec((1, xw, m), lambda i: (i, 0, 0)),
                   pl.BlockSpec((1, m, xw), lambda i: (i, 0, 0))],
        out_shape=[jax.ShapeDtypeStruct((b, xw, m), BF16), jax.ShapeDtypeStruct((b, m, xw), BF16)],
        compiler_params=_cparams(("parallel",)),
        name="kv",
    )(mem, wkt, wv)


def _gmlp_kernel(u_ref, v_ref, w_ref, bt_ref, y_ref, *, n_chunks):
    c = GMLP_CHUNK
    rows = lax.broadcasted_iota(jnp.int32, (c, c), 0)
    cols = lax.broadcasted_iota(jnp.int32, (c, c), 1)
    causal = cols <= rows
    for g in range(GMLP_GROUPS):
        wg = jnp.where(causal, w_ref[g], 0.0).astype(BF16)
        bcol = bt_ref[:, g:g + 1]
        for n in range(n_chunks):
            vb = v_ref[n * c:(n + 1) * c, g * c:(g + 1) * c]
            mixed = _dot(wg, vb) + bcol
            ub = u_ref[n * c:(n + 1) * c, g * c:(g + 1) * c].astype(F32)
            y_ref[n * c:(n + 1) * c, g * c:(g + 1) * c] = (ub * mixed).astype(BF16)


def _gmlp(u, v, w_sp, b_sp_t, ts):
    t, d = u.shape
    row = pl.BlockSpec((ts, d), lambda i: (i, 0))
    return pl.pallas_call(
        functools.partial(_gmlp_kernel, n_chunks=ts // GMLP_CHUNK),
        grid=(t // ts,),
        in_specs=[row, row,
                  pl.BlockSpec(w_sp.shape, lambda i: (0, 0, 0)),
                  pl.BlockSpec(b_sp_t.shape, lambda i: (0, 0))],
        out_specs=row,
        out_shape=jax.ShapeDtypeStruct((t, d), BF16),
        compiler_params=_cparams(("parallel",)),
        name="gmlp",
    )(u, v, w_sp, b_sp_t)


def _lru_kernel(lx_ref, gl_ref, cw_ref, cb_ref, wai_ref, ba_ref, bi_ref, lam_ref, y_ref,
                xe_ref, hc_ref, *, ts, bd):
    s = pl.program_id(1)

    @pl.when(s == 0)
    def _():
        xe_ref[0:8, :] = jnp.zeros((8, xe_ref.shape[1]), F32)
        hc_ref[...] = jnp.zeros(hc_ref.shape, F32)

    xe_ref[8:8 + ts, :] = lx_ref[...].astype(F32)
    row = lax.broadcasted_iota(jnp.int32, (ts, bd), 0)

    for blk in range(LRU_BLOCKS):
        cs = slice(blk * bd, (blk + 1) * bd)
        xc = cb_ref[:, cs] + jnp.zeros((ts, bd), F32)
        for k in range(CONV_WIDTH):
            xc = xc + cw_ref[k:k + 1, cs] * xe_ref[8 - k:8 - k + ts, cs]
        ri = _dot(xc.astype(BF16), wai_ref[blk])
        r = _sigmoid(ri[:, :bd] + ba_ref[:, cs])
        ig = _sigmoid(ri[:, bd:] + bi_ref[:, cs])
        lam = lam_ref[:, cs]
        softplus_neg = jnp.maximum(-lam, 0.0) + jnp.log1p(jnp.exp(-jnp.abs(lam)))
        log_a = (-LRU_C) * r * softplus_neg
        a = jnp.exp(log_a)
        mult = jnp.sqrt(1.0 - jnp.exp(2.0 * log_a))
        b = mult * (ig * xc)
        dsh = 1
        while dsh < ts:
            a_sh = pltpu.roll(a, dsh, 0)
            b_sh = pltpu.roll(b, dsh, 0)
            keep = row >= dsh
            b = jnp.where(keep, a * b_sh + b, b)
            a = jnp.where(keep, a * a_sh, a)
            dsh *= 2
        h = b + a * hc_ref[0:1, cs]
        hc_ref[0:1, cs] = h[ts - 1:ts, :]
        y_ref[:, cs] = (h * gl_ref[:, cs].astype(F32)).astype(BF16)

    xe_ref[0:8, :] = xe_ref[ts:ts + 8, :]


def _lru(lx, gl, conv_w, conv_b, wai, ba, bi, lam, batch, ts):
    t, d = lx.shape
    n_s = t // batch // ts
    bd = d // LRU_BLOCKS
    row = pl.BlockSpec((ts, d), lambda b, s: (b * n_s + s, 0))
    vec = pl.BlockSpec((1, d), lambda b, s: (0, 0))
    return pl.pallas_call(
        functools.partial(_lru_kernel, ts=ts, bd=bd),
        grid=(batch, n_s),
        in_specs=[row, row,
                  pl.BlockSpec(conv_w.shape, lambda b, s: (0, 0)), vec,
                  pl.BlockSpec(wai.shape, lambda b, s: (0, 0, 0)), vec, vec, vec],
        out_specs=row,
        out_shape=jax.ShapeDtypeStruct((t, d), BF16),
        scratch_shapes=[pltpu.VMEM((ts + 8, d), F32), pltpu.VMEM((8, d), F32)],
        compiler_params=_cparams(("arbitrary", "arbitrary")),
        name="lru",
    )(lx, gl, conv_w, conv_b, wai, ba, bi, lam)


def _xattn_kernel(q_ref, kt_ref, v_ref, y_ref, *, hd):
    for h in range(XA_HEADS):
        cs = slice(h * hd, (h + 1) * hd)
        sc = _dot(q_ref[:, cs], kt_ref[0, cs, :])
        m = jnp.max(sc, axis=-1, keepdims=True)
        e = jnp.exp(sc - m)
        l = jnp.sum(e, axis=-1, keepdims=True)
        o = _dot(e.astype(BF16), v_ref[0, :, cs])
        y_ref[:, cs] = (o / l).astype(BF16)


def _xattn(q, kt, v, batch, ts):
    t, d = q.shape
    n_s = t // batch // ts
    row = pl.BlockSpec((ts, d), lambda b, s: (b * n_s + s, 0))
    return pl.pallas_call(
        functools.partial(_xattn_kernel, hd=d // XA_HEADS),
        grid=(batch, n_s),
        in_specs=[row,
                  pl.BlockSpec((1,) + kt.shape[1:], lambda b, s: (b, 0, 0)),
                  pl.BlockSpec((1,) + v.shape[1:], lambda b, s: (b, 0, 0))],
        out_specs=row,
        out_shape=jax.ShapeDtypeStruct((t, d), BF16),
        compiler_params=_cparams(("parallel", "parallel")),
        name="xattn",
    )(q, kt, v)


def _merge_kernel(x_ref, lg_ref, lb_ref, yg_ref, yl_ref, yx_ref, g0_ref, g1_ref, g2_ref,
                  w0_ref, w1_ref, w2_ref, wo_ref, l1g_ref, l1b_ref, wrt_ref, h1_ref, lt_ref):
    h0 = _ln(x_ref[...], lg_ref[...], lb_ref[...])
    merged = (g0_ref[...].astype(F32) * _dot(yg_ref[...], w0_ref[...])
              + g1_ref[...].astype(F32) * _dot(yl_ref[...], w1_ref[...])
              + g2_ref[...].astype(F32) * _dot(yx_ref[...], w2_ref[...]))
    h1 = _ln(DN_ALPHA * h0 + _dot(merged.astype(BF16), wo_ref[...]), l1g_ref[...], l1b_ref[...])
    h1_ref[...] = h1
    lt_ref[...] = lax.dot_general(wrt_ref[...], h1, (((1,), (1,)), ((), ())),
                                  precision=lax.Precision.HIGHEST, preferred_element_type=F32)


def _merge(x2, ln_g, ln_b, yg, yl, yx, g0, g1, g2, w0, w1, w2, wo, l1g, l1b, wrt, tm):
    t, d = x2.shape
    e = wrt.shape[0]
    row = pl.BlockSpec((tm, d), lambda i: (i, 0))
    vec = pl.BlockSpec((1, d), lambda i: (0, 0))
    wsq = pl.BlockSpec((d, d), lambda i: (0, 0))
    return pl.pallas_call(
        _merge_kernel,
        grid=(t // tm,),
        in_specs=[row, vec, vec, row, row, row, row, row, row, wsq, wsq, wsq, wsq, vec, vec,
                  pl.BlockSpec(wrt.shape, lambda i: (0, 0))],
        out_specs=[row, pl.BlockSpec((e, tm), lambda i: (0, i))],
        out_shape=[jax.ShapeDtypeStruct((t, d), F32), jax.ShapeDtypeStruct((e, t), F32)],
        compiler_params=_cparams(("parallel",)),
        name="merge",
    )(x2, ln_g, ln_b, yg, yl, yx, g0, g1, g2, w0, w1, w2, wo, l1g, l1b, wrt)


def _first_argmax(x, rowf, n):
    m = jnp.max(x, axis=0, keepdims=True)
    first = jnp.min(jnp.where(x == m, rowf, float(n)), axis=0, keepdims=True)
    return m, first


def _route_kernel(lt_ref, rb_ref, tri_ref, idx_ref, wk_ref, rank_ref, cnt_ref, run_ref, *, tr):
    @pl.when(pl.program_id(0) == 0)
    def _():
        run_ref[...] = jnp.zeros(run_ref.shape, F32)

    gsz = N_EXPERTS // N_GROUPS
    s = _sigmoid(lt_ref[...])
    choice = s + rb_ref[:, 0:1]
    neg_inf = -jnp.inf

    rowg = lax.broadcasted_iota(jnp.int32, (gsz, tr), 0).astype(F32)
    gs_rows = []
    for g in range(N_GROUPS):
        cg = choice[g * gsz:(g + 1) * gsz, :]
        m1, f1 = _first_argmax(cg, rowg, gsz)
        m2 = jnp.max(jnp.where(rowg == f1, neg_inf, cg), axis=0, keepdims=True)
        gs_rows.append(m1 + m2)
    gscore = jnp.concatenate(gs_rows, axis=0)

    rowG = lax.broadcasted_iota(jnp.int32, (N_GROUPS, tr), 0).astype(F32)
    gmask = jnp.zeros((N_GROUPS, tr), F32)
    for _ in range(TOPK_GROUPS):
        _, f = _first_argmax(gscore, rowG, N_GROUPS)
        hit = rowG == f
        gmask = jnp.where(hit, 1.0, gmask)
        gscore = jnp.where(hit, neg_inf, gscore)
    emask = jnp.concatenate(
        [jnp.broadcast_to(gmask[g:g + 1, :], (gsz, tr)) for g in range(N_GROUPS)], axis=0)

    rowE = lax.broadcasted_iota(jnp.int32, (N_EXPERTS, tr), 0).astype(F32)
    masked = jnp.where(emask > 0.5, choice, neg_inf)
    hits = []
    self32 = jnp.zeros((N_EXPERTS, tr), F32)
    for _ in range(TOP_K):
        _, f = _first_argmax(masked, rowE, N_EXPERTS)
        hit = rowE == f
        hits.append((hit, f))
        self32 = jnp.where(hit, 1.0, self32)
        masked = jnp.where(hit, neg_inf, masked)

    wsel = self32 * s
    denom = jnp.sum(wsel, axis=0, keepdims=True)
    w = wsel / denom * ROUTED_SCALE

    pos = _dot(self32.astype(BF16), tri_ref[...]) + run_ref[:, 0:1]
    run_new = run_ref[:, 0:1] + jnp.sum(self32, axis=1, keepdims=True)
    run_ref[...] = jnp.broadcast_to(run_new, run_ref.shape)
    cnt_ref[...] = jnp.broadcast_to(run_new, cnt_ref.shape).astype(jnp.int32)

    for k, (hit, f) in enumerate(hits):
        idx_ref[k:k + 1, :] = f.astype(jnp.int32)
        wk_ref[k:k + 1, :] = jnp.sum(jnp.where(hit, w, 0.0), axis=0, keepdims=True)
        rank_ref[k:k + 1, :] = jnp.sum(jnp.where(hit, pos, 0.0), axis=0, keepdims=True).astype(jnp.int32)


def _route(lt, rb, tri, tr):
    e, t = lt.shape
    kblk = pl.BlockSpec((TOP_K, tr), lambda i: (0, i))
    return pl.pallas_call(
        functools.partial(_route_kernel, tr=tr),
        grid=(t // tr,),
        in_specs=[pl.BlockSpec((e, tr), lambda i: (0, i)),
                  pl.BlockSpec(rb.shape, lambda i: (0, 0)),
                  pl.BlockSpec(tri.shape, lambda i: (0, 0))],
        out_specs=[kblk, kblk, kblk, pl.BlockSpec((e, 128), lambda i: (0, 0))],
        out_shape=[jax.ShapeDtypeStruct((TOP_K, t), jnp.int32),
                   jax.ShapeDtypeStruct((TOP_K, t), F32),
                   jax.ShapeDtypeStruct((TOP_K, t), jnp.int32),
                   jax.ShapeDtypeStruct((e, 128), jnp.int32)],
        scratch_shapes=[pltpu.VMEM((e, 128), F32)],
        compiler_params=_cparams(("arbitrary",)),
        name="route",
    )(lt, rb, tri)


def _dest_kernel(ps_ref, idx_ref, rank_ref, dest_ref):
    idx = idx_ref[...]
    base = jnp.zeros(idx.shape, jnp.int32)
    for e in range(N_EXPERTS):
        base = jnp.where(idx == e, ps_ref[e], base)
    dest_ref[...] = base + rank_ref[...]


def _dest(pad_start, idx, rank, tb):
    k, t = idx.shape
    blk = pl.BlockSpec((k, tb), lambda i, ps: (0, i))
    return pl.pallas_call(
        _dest_kernel,
        grid_spec=pltpu.PrefetchScalarGridSpec(
            num_scalar_prefetch=1, grid=(t // tb,), in_specs=[blk, blk], out_specs=blk),
        out_shape=jax.ShapeDtypeStruct((k, t), jnp.int32),
        compiler_params=_cparams(("parallel",)),
        name="dest",
    )(pad_start, idx, rank)


def _dispatch_kernel(zb_ref, dest_ref, h1_ref, xs_ref, zbuf, sem, zsem, *, td):
    @pl.when(pl.program_id(0) == 0)
    def _():
        zbuf[...] = jnp.zeros(zbuf.shape, F32)

        def zcopy(j):
            return pltpu.make_async_copy(
                zbuf, xs_ref.at[pl.ds(zb_ref[j] * EXPERT_BLOCK, EXPERT_BLOCK), :], zsem)

        def zstart(j, carry):
            @pl.when(zb_ref[j] >= 0)
            def _():
                zcopy(j).start()
            return carry

        def zwait(j, carry):
            @pl.when(zb_ref[j] >= 0)
            def _():
                zcopy(j).wait()
            return carry

        lax.fori_loop(0, zb_ref.shape[0], zstart, 0)
        lax.fori_loop(0, zb_ref.shape[0], zwait, 0)

    def body(t, carry):
        for k in range(TOP_K):
            pltpu.make_async_copy(h1_ref.at[pl.ds(t, 1), :],
                                  xs_ref.at[pl.ds(dest_ref[k, t], 1), :], sem).start()
        return carry

    lax.fori_loop(0, td, body, 0)
    for _ in range(TOP_K):
        pltpu.make_async_copy(h1_ref, xs_ref.at[pl.ds(0, td), :], sem).wait()


def _dispatch(zero_blocks, dest, h1, n_rows, td):
    t, d = h1.shape
    return pl.pallas_call(
        functools.partial(_dispatch_kernel, td=td),
        grid_spec=pltpu.PrefetchScalarGridSpec(
            num_scalar_prefetch=1, grid=(t // td,),
            in_specs=[pl.BlockSpec((TOP_K, td), lambda i, zb: (0, i), memory_space=pltpu.SMEM),
                      pl.BlockSpec((td, d), lambda i, zb: (i, 0))],
            out_specs=pl.BlockSpec(memory_space=pl.ANY),
            scratch_shapes=[pltpu.VMEM((EXPERT_BLOCK, d), F32),
                            pltpu.SemaphoreType.DMA, pltpu.SemaphoreType.DMA]),
        out_shape=jax.ShapeDtypeStruct((n_rows, d), F32),
        compiler_params=_cparams(("arbitrary",)),
        name="dispatch",
    )(zero_blocks, dest, h1)


def _expert_kernel(be_ref, nu_ref, xs_ref, wg_ref, wu_ref, wd_ref, ys_ref):
    i = pl.program_id(0)

    @pl.when(i < nu_ref[0])
    def _():
        x = xs_ref[...].astype(BF16)
        g = _dot(x, wg_ref[0].astype(BF16))
        u = _dot(x, wu_ref[0].astype(BF16))
        act = (g * _sigmoid(g) * u).astype(BF16)
        ys_ref[...] = _dot(act, wd_ref[0].astype(BF16))

    @pl.when(i >= nu_ref[0])
    def _():
        ys_ref[...] = jnp.zeros(ys_ref.shape, F32)


def _experts(block_e, n_used, xs, wg, wu, wd):
    n_rows, d = xs.shape
    n_blocks = n_rows // EXPERT_BLOCK
    ed = wg.shape[2]

    def xmap(i, be, nu):
        return (jnp.minimum(i, nu[0] - 1), 0)

    def ymap(i, be, nu):
        return (i, 0)

    def wmap(i, be, nu):
        return (be[i], 0, 0)

    return pl.pallas_call(
        _expert_kernel,
        grid_spec=pltpu.PrefetchScalarGridSpec(
            num_scalar_prefetch=2, grid=(n_blocks,),
            in_specs=[pl.BlockSpec((EXPERT_BLOCK, d), xmap),
                      pl.BlockSpec((1, d, ed), wmap),
                      pl.BlockSpec((1, d, ed), wmap),
                      pl.BlockSpec((1, ed, d), wmap)],
            out_specs=pl.BlockSpec((EXPERT_BLOCK, d), ymap)),
        out_shape=jax.ShapeDtypeStruct((n_rows, d), F32),
        compiler_params=_cparams(("arbitrary",)),
        name="experts",
    )(block_e, n_used, xs, wg, wu, wd)


def _combine_kernel(dest_ref, h1_ref, wk_ref, ys_ref, sgu_ref, sd_ref, l2g_ref, l2b_ref, out_ref,
                    gbuf, sem, *, tc, sdim):
    def body(t, carry):
        for k in range(TOP_K):
            pltpu.make_async_copy(ys_ref.at[pl.ds(dest_ref[k, t], 1), :],
                                  gbuf.at[k, pl.ds(t, 1), :], sem).start()
        return carry

    lax.fori_loop(0, tc, body, 0)

    h1 = h1_ref[...]
    gu = _dot(h1.astype(BF16), sgu_ref[...])
    g = gu[:, :sdim]
    act = (g * _sigmoid(g) * gu[:, sdim:]).astype(BF16)
    acc = DN_ALPHA * h1 + _dot(act, sd_ref[...])

    for k in range(TOP_K):
        pltpu.make_async_copy(ys_ref.at[pl.ds(0, tc), :], gbuf.at[k], sem).wait()
    for k in range(TOP_K):
        acc = acc + wk_ref[:, k:k + 1] * gbuf[k]
    out_ref[...] = _ln(acc, l2g_ref[...], l2b_ref[...])


def _combine(dest, h1, wk, ys, sgu, sd, l2g, l2b, tc):
    t, d = h1.shape
    sdim = sd.shape[0]
    row = pl.BlockSpec((tc, d), lambda i: (i, 0))
    vec = pl.BlockSpec((1, d), lambda i: (0, 0))
    return pl.pallas_call(
        functools.partial(_combine_kernel, tc=tc, sdim=sdim),
        grid=(t // tc,),
        in_specs=[pl.BlockSpec((TOP_K, tc), lambda i: (0, i), memory_space=pltpu.SMEM),
                  row,
                  pl.BlockSpec((tc, TOP_K), lambda i: (i, 0)),
                  pl.BlockSpec(memory_space=pl.ANY),
                  pl.BlockSpec(sgu.shape, lambda i: (0, 0)),
                  pl.BlockSpec(sd.shape, lambda i: (0, 0)),
                  vec, vec],
        out_specs=row,
        out_shape=jax.ShapeDtypeStruct((t, d), F32),
        scratch_shapes=[pltpu.VMEM((TOP_K, tc, d), F32), pltpu.SemaphoreType.DMA],
        compiler_params=_cparams(("arbitrary",)),
        name="combine",
    )(dest, h1, wk, ys, sgu, sd, l2g, l2b)


def _tile(n, pref):
    while n % pref:
        pref //= 2
    return pref


def kernel(x, mem, ln_in_g, ln_in_b, w_in, gmlp_norm_g, gmlp_norm_b, gmlp_spatial_w, gmlp_spatial_b, conv_w, conv_b, lru_wa, lru_ba, lru_wi, lru_bi, lru_lambda, w_kv, w_br_gmlp, w_br_lru, w_br_xa, w_out, ln1_g, ln1_b, w_router, router_bias, exp_gate, exp_up, exp_down, sh_gate, sh_up, sh_down, ln2_g, ln2_b):
    b, s, d = x.shape
    t = b * s
    assert w_in.shape[0] == DEPTH and w_in.shape[2] == 8 * d and s % GMLP_CHUNK == 0
    r2 = lambda a: a.reshape(1, -1)
    x2 = x.reshape(t, d)
    lg, lb = r2(ln_in_g), r2(ln_in_b)
    hd = d // XA_HEADS

    tm = _tile(t, 512)
    u, v, lx, gl, q, g0, g1, g2 = _inproj(
        x2, lg, lb, w_in[0].astype(BF16), r2(gmlp_norm_g[0]), r2(gmlp_norm_b[0]), tm, hd ** -0.5)

    wkv = w_kv[0]
    kt, vv = _kv(mem, wkv[:, :d].T.astype(BF16), wkv[:, d:].astype(BF16))

    yg = _gmlp(u, v, gmlp_spatial_w[0], gmlp_spatial_b[0].T, _tile(s, 512))

    wai = jnp.concatenate([lru_wa[0], lru_wi[0]], axis=-1).astype(BF16)
    yl = _lru(lx, gl, conv_w[0], r2(conv_b[0]), wai, r2(lru_ba[0]), r2(lru_bi[0]), r2(lru_lambda[0]),
              b, _tile(s, 256))

    yx = _xattn(q, kt, vv, b, _tile(s, 512))

    h1, lt = _merge(x2, lg, lb, yg, yl, yx, g0, g1, g2,
                    w_br_gmlp[0].astype(BF16), w_br_lru[0].astype(BF16), w_br_xa[0].astype(BF16),
                    w_out[0].astype(BF16), r2(ln1_g[0]), r2(ln1_b[0]), w_router[0].T, tm)

    tr = _tile(t, 512)
    tri = (lax.broadcasted_iota(jnp.int32, (tr, tr), 0)
           < lax.broadcasted_iota(jnp.int32, (tr, tr), 1)).astype(BF16)
    rb = jnp.broadcast_to(router_bias[0].astype(F32)[:, None], (N_EXPERTS, 128))
    idx, wk, rank, cnt = _route(lt, rb, tri, tr)

    counts = cnt[:, 0]
    padded = (counts + EXPERT_BLOCK - 1) // EXPERT_BLOCK * EXPERT_BLOCK
    pad_end = jnp.cumsum(padded)
    pad_start = pad_end - padded
    n_blocks = -(-(t * TOP_K) // EXPERT_BLOCK) + N_EXPERTS
    block_start = jnp.arange(n_blocks, dtype=jnp.int32) * EXPERT_BLOCK
    block_e = jnp.minimum(jnp.searchsorted(pad_end, block_start, side='right'),
                          N_EXPERTS - 1).astype(jnp.int32)
    n_used = (pad_end[-1:] // EXPERT_BLOCK).astype(jnp.int32)

    last_blk = pad_end // EXPERT_BLOCK - 1
    tail_blk = n_used[0] + jnp.arange(N_EXPERTS, dtype=jnp.int32)
    zero_blocks = jnp.concatenate([
        jnp.where(counts % EXPERT_BLOCK != 0, last_blk, -1),
        jnp.where(tail_blk < n_blocks, tail_blk, -1)]).astype(jnp.int32)

    dest = _dest(pad_start.astype(jnp.int32), idx, rank, _tile(t, 2048))
    td = _tile(t, 256)
    xs = _dispatch(zero_blocks, dest, h1, n_blocks * EXPERT_BLOCK, td)
    ys = _experts(block_e, n_used, xs, exp_gate[0], exp_up[0], exp_down[0])
    sgu = jnp.concatenate([sh_gate[0], sh_up[0]], axis=-1).astype(BF16)
    out = _combine(dest, h1, wk.T, ys, sgu, sh_down[0].astype(BF16), r2(ln2_g[0]), r2(ln2_b[0]), td)
    return out.reshape(b, s, d)
```

```python
import functools

import jax
import jax.numpy as jnp
from jax import lax
from jax.experimental import pallas as pl
from jax.experimental.pallas import tpu as pltpu

F32 = jnp.float32
BF16 = jnp.bfloat16

GMLP_GROUPS = 8
GMLP_CHUNK = 128
LRU_BLOCKS = 8
CONV_WIDTH = 4
LRU_C = 8.0
XA_HEADS = 4
N_EXPERTS = 64
TOP_K = 8
N_GROUPS = 8
TOPK_GROUPS = 4
ROUTED_SCALE = 2.5
EXPERT_BLOCK = 512
ROUTE_TILE = 256
RUN_ALIGN = 16
SORT_CHUNK = 256
LN_EPS = 1e-5
DEPTH = 1
DN_ALPHA = (2.0 * DEPTH) ** 0.25

VMEM_LIMIT_V7X = 56 * 1024 * 1024


def _cparams(sem):
    return pltpu.CompilerParams(dimension_semantics=sem, vmem_limit_bytes=VMEM_LIMIT_V7X)


def _ln(x, g, b):
    mu = jnp.mean(x, axis=-1, keepdims=True)
    xc = x - mu
    var = jnp.mean(xc * xc, axis=-1, keepdims=True)
    return xc * lax.rsqrt(var + LN_EPS) * g + b


def _gelu(x):
    return x * (0.5 * (1.0 + jnp.tanh(0.7978845608028654 * (x + 0.044715 * (x * x * x)))))


def _sigmoid(x):
    return 0.5 * (1.0 + jnp.tanh(0.5 * x))


def _dot(a, b):
    return jnp.dot(a, b, preferred_element_type=F32)


def _inproj_kernel(x_ref, lg_ref, lb_ref, w_ref, ng_ref, nb_ref,
                   u_ref, v_ref, lx_ref, gl_ref, q_ref, g0_ref, g1_ref, g2_ref, *, d, q_scale):
    hb = _ln(x_ref[...], lg_ref[...], lb_ref[...]).astype(BF16)

    def z(j):
        return _dot(hb, w_ref[:, j * d:(j + 1) * d])

    u_ref[...] = _gelu(z(0)).astype(BF16)
    v_ref[...] = _ln(_gelu(z(1)), ng_ref[...], nb_ref[...]).astype(BF16)
    lx_ref[...] = z(2).astype(BF16)
    gl_ref[...] = _gelu(z(3)).astype(BF16)
    q_ref[...] = (z(4) * q_scale).astype(BF16)
    g0_ref[...] = _sigmoid(z(5)).astype(BF16)
    g1_ref[...] = _sigmoid(z(6)).astype(BF16)
    g2_ref[...] = _sigmoid(z(7)).astype(BF16)


def _inproj(x2, ln_g, ln_b, w_in, ng, nb, tm, q_scale):
    t, d = x2.shape
    n_out = 8
    row = pl.BlockSpec((tm, d), lambda i: (i, 0))
    vec = pl.BlockSpec((1, d), lambda i: (0, 0))
    wspec = pl.BlockSpec(w_in.shape, lambda i: (0, 0), pipeline_mode=pl.Buffered(1))
    return pl.pallas_call(
        functools.partial(_inproj_kernel, d=d, q_scale=q_scale),
        grid=(t // tm,),
        in_specs=[row, vec, vec, wspec, vec, vec],
        out_specs=[row] * n_out,
        out_shape=[jax.ShapeDtypeStruct((t, d), BF16)] * n_out,
        compiler_params=_cparams(("parallel",)),
        name="inproj",
    )(x2, ln_g, ln_b, w_in, ng, nb)


def _kv_kernel(mem_ref, wkt_ref, wv_ref, kt_ref, v_ref):
    m = mem_ref[0].astype(BF16)
    kt = lax.dot_general(wkt_ref[...], m, (((1,), (1,)), ((), ())), preferred_element_type=F32)
    kt_ref[0] = kt.astype(BF16)
    v_ref[0] = _dot(m, wv_ref[...]).astype(BF16)


def _kv(mem, wkt, wv):
    b, m, d = mem.shape
    xw = wv.shape[1]
    return pl.pallas_call(
        _kv_kernel,
        grid=(b,),
        in_specs=[pl.BlockSpec((1, m, d), lambda i: (i, 0, 0)),
                  pl.BlockSpec(wkt.shape, lambda i: (0, 0)),
                  pl.BlockSpec(wv.shape, lambda i: (0, 0))],
        out_specs=[pl.BlockSpec((1, xw, m), lambda i: (i, 0, 0)),
                   pl.BlockSpec((1, m, xw), lambda i: (i, 0, 0))],
        out_shape=[jax.ShapeDtypeStruct((b, xw, m), BF16), jax.ShapeDtypeStruct((b, m, xw), BF16)],
        compiler_params=_cparams(("parallel",)),
        name="kv",
    )(mem, wkt, wv)


def _gmlp_kernel(u_ref, v_ref, w_ref, bt_ref, y_ref, *, n_chunks):
    c = GMLP_CHUNK
    rows = lax.broadcasted_iota(jnp.int32, (c, c), 0)
    cols = lax.broadcasted_iota(jnp.int32, (c, c), 1)
    causal = cols <= rows
    for g in range(GMLP_GROUPS):
        wg = jnp.where(causal, w_ref[g], 0.0).astype(BF16)
        bcol = bt_ref[:, g:g + 1]
        for n in range(n_chunks):
            vb = v_ref[n * c:(n + 1) * c, g * c:(g + 1) * c]
            mixed = _dot(wg, vb) + bcol
            ub = u_ref[n * c:(n + 1) * c, g * c:(g + 1) * c].astype(F32)
            y_ref[n * c:(n + 1) * c, g * c:(g + 1) * c] = (ub * mixed).astype(BF16)


def _gmlp(u, v, w_sp, b_sp_t, ts):
    t, d = u.shape
    row = pl.BlockSpec((ts, d), lambda i: (i, 0))
    return pl.pallas_call(
        functools.partial(_gmlp_kernel, n_chunks=ts // GMLP_CHUNK),
        grid=(t // ts,),
        in_specs=[row, row,
                  pl.BlockSpec(w_sp.shape, lambda i: (0, 0, 0)),
                  pl.BlockSpec(b_sp_t.shape, lambda i: (0, 0))],
        out_specs=row,
        out_shape=jax.ShapeDtypeStruct((t, d), BF16),
        compiler_params=_cparams(("parallel",)),
        name="gmlp",
    )(u, v, w_sp, b_sp_t)


def _lru_kernel(lx_ref, gl_ref, cw_ref, cb_ref, wai_ref, ba_ref, bi_ref, lam_ref, y_ref,
                xe_ref, hc_ref, *, ts, bd):
    s = pl.program_id(1)

    @pl.when(s == 0)
    def _():
        xe_ref[0:8, :] = jnp.zeros((8, xe_ref.shape[1]), F32)
        hc_ref[...] = jnp.zeros(hc_ref.shape, F32)

    xe_ref[8:8 + ts, :] = lx_ref[...].astype(F32)
    row = lax.broadcasted_iota(jnp.int32, (ts, bd), 0)

    for blk in range(LRU_BLOCKS):
        cs = slice(blk * bd, (blk + 1) * bd)
        xc = cb_ref[:, cs] + jnp.zeros((ts, bd), F32)
        for k in range(CONV_WIDTH):
            xc = xc + cw_ref[k:k + 1, cs] * xe_ref[8 - k:8 - k + ts, cs]
        ri = _dot(xc.astype(BF16), wai_ref[blk])
        r = _sigmoid(ri[:, :bd] + ba_ref[:, cs])
        ig = _sigmoid(ri[:, bd:] + bi_ref[:, cs])
        lam = lam_ref[:, cs]
        softplus_neg = jnp.maximum(-lam, 0.0) + jnp.log1p(jnp.exp(-jnp.abs(lam)))
        log_a = (-LRU_C) * r * softplus_neg
        a = jnp.exp(log_a)
        mult = jnp.sqrt(1.0 - jnp.exp(2.0 * log_a))
        b = mult * (ig * xc)
        dsh = 1
        while dsh < ts:
            a_sh = pltpu.roll(a, dsh, 0)
            b_sh = pltpu.roll(b, dsh, 0)
            keep = row >= dsh
            b = jnp.where(keep, a * b_sh + b, b)
            a = jnp.where(keep, a * a_sh, a)
            dsh *= 2
        h = b + a * hc_ref[0:1, cs]
        hc_ref[0:1, cs] = h[ts - 1:ts, :]
        y_ref[:, cs] = (h * gl_ref[:, cs].astype(F32)).astype(BF16)

    xe_ref[0:8, :] = xe_ref[ts:ts + 8, :]


def _lru(lx, gl, conv_w, conv_b, wai, ba, bi, lam, batch, ts):
    t, d = lx.shape
    n_s = t // batch // ts
    bd = d // LRU_BLOCKS
    row = pl.BlockSpec((ts, d), lambda b, s: (b * n_s + s, 0))
    vec = pl.BlockSpec((1, d), lambda b, s: (0, 0))
    return pl.pallas_call(
        functools.partial(_lru_kernel, ts=ts, bd=bd),
        grid=(batch, n_s),
        in_specs=[row, row,
                  pl.BlockSpec(conv_w.shape, lambda b, s: (0, 0)), vec,
                  pl.BlockSpec(wai.shape, lambda b, s: (0, 0, 0)), vec, vec, vec],
        out_specs=row,
        out_shape=jax.ShapeDtypeStruct((t, d), BF16),
        scratch_shapes=[pltpu.VMEM((ts + 8, d), F32), pltpu.VMEM((8, d), F32)],
        compiler_params=_cparams(("arbitrary", "arbitrary")),
        name="lru",
    )(lx, gl, conv_w, conv_b, wai, ba, bi, lam)


def _xattn_kernel(q_ref, kt_ref, v_ref, y_ref, *, hd):
    for h in range(XA_HEADS):
        cs = slice(h * hd, (h + 1) * hd)
        sc = _dot(q_ref[:, cs], kt_ref[0, cs, :])
        m = jnp.max(sc, axis=-1, keepdims=True)
        e = jnp.exp(sc - m)
        l = jnp.sum(e, axis=-1, keepdims=True)
        o = _dot(e.astype(BF16), v_ref[0, :, cs])
        y_ref[:, cs] = (o / l).astype(BF16)


def _xattn(q, kt, v, batch, ts):
    t, d = q.shape
    n_s = t // batch // ts
    row = pl.BlockSpec((ts, d), lambda b, s: (b * n_s + s, 0))
    return pl.pallas_call(
        functools.partial(_xattn_kernel, hd=d // XA_HEADS),
        grid=(batch, n_s),
        in_specs=[row,
                  pl.BlockSpec((1,) + kt.shape[1:], lambda b, s: (b, 0, 0)),
                  pl.BlockSpec((1,) + v.shape[1:], lambda b, s: (b, 0, 0))],
        out_specs=row,
        out_shape=jax.ShapeDtypeStruct((t, d), BF16),
        compiler_params=_cparams(("parallel", "parallel")),
        name="xattn",
    )(q, kt, v)


def _merge_kernel(x_ref, lg_ref, lb_ref, yg_ref, yl_ref, yx_ref, g0_ref, g1_ref, g2_ref,
                  w0_ref, w1_ref, w2_ref, wo_ref, l1g_ref, l1b_ref, wrt_ref, h1_ref, lt_ref):
    h0 = _ln(x_ref[...], lg_ref[...], lb_ref[...])
    merged = (g0_ref[...].astype(F32) * _dot(yg_ref[...], w0_ref[...])
              + g1_ref[...].astype(F32) * _dot(yl_ref[...], w1_ref[...])
              + g2_ref[...].astype(F32) * _dot(yx_ref[...], w2_ref[...]))
    h1 = _ln(DN_ALPHA * h0 + _dot(merged.astype(BF16), wo_ref[...]), l1g_ref[...], l1b_ref[...])
    h1_ref[...] = h1
    lt_ref[...] = lax.dot_general(wrt_ref[...], h1, (((1,), (1,)), ((), ())),
                                  precision=lax.Precision.HIGHEST, preferred_element_type=F32)


def _merge(x2, ln_g, ln_b, yg, yl, yx, g0, g1, g2, w0, w1, w2, wo, l1g, l1b, wrt, tm):
    t, d = x2.shape
    e = wrt.shape[0]
    row = pl.BlockSpec((tm, d), lambda i: (i, 0))
    vec = pl.BlockSpec((1, d), lambda i: (0, 0))
    wsq = pl.BlockSpec((d, d), lambda i: (0, 0))
    return pl.pallas_call(
        _merge_kernel,
        grid=(t // tm,),
        in_specs=[row, vec, vec, row, row, row, row, row, row, wsq, wsq, wsq, wsq, vec, vec,
                  pl.BlockSpec(wrt.shape, lambda i: (0, 0))],
        out_specs=[row, pl.BlockSpec((e, tm), lambda i: (0, i))],
        out_shape=[jax.ShapeDtypeStruct((t, d), F32), jax.ShapeDtypeStruct((e, t), F32)],
        compiler_params=_cparams(("parallel",)),
        name="merge",
    )(x2, ln_g, ln_b, yg, yl, yx, g0, g1, g2, w0, w1, w2, wo, l1g, l1b, wrt)


def _first_argmax(x, rowf, n):
    m = jnp.max(x, axis=0, keepdims=True)
    first = jnp.min(jnp.where(x == m, rowf, float(n)), axis=0, keepdims=True)
    return m, first


def _strict_lower(n, m, transpose=False):
    r = lax.broadcasted_iota(jnp.int32, (n, m), 0)
    c = lax.broadcasted_iota(jnp.int32, (n, m), 1)
    return jnp.where((r < c) if transpose else (c < r), 1.0, 0.0).astype(BF16)


def _route_kernel(lt_ref, rb_ref, pos_ref, wk_ref, len_ref, off_ref, run_ref, *, tr):
    @pl.when(pl.program_id(0) == 0)
    def _():
        run_ref[...] = jnp.zeros(run_ref.shape, F32)

    gsz = N_EXPERTS // N_GROUPS
    s = _sigmoid(lt_ref[...])
    choice = s + rb_ref[:, 0:1]
    neg_inf = -jnp.inf

    rowg = lax.broadcasted_iota(jnp.int32, (gsz, tr), 0).astype(F32)
    gs_rows = []
    for g in range(N_GROUPS):
        cg = choice[g * gsz:(g + 1) * gsz, :]
        m1, f1 = _first_argmax(cg, rowg, gsz)
        m2 = jnp.max(jnp.where(rowg == f1, neg_inf, cg), axis=0, keepdims=True)
        gs_rows.append(m1 + m2)
    gscore = jnp.concatenate(gs_rows, axis=0)

    rowG = lax.broadcasted_iota(jnp.int32, (N_GROUPS, tr), 0).astype(F32)
    gmask = jnp.zeros((N_GROUPS, tr), F32)
    for _ in range(TOPK_GROUPS):
        _, f = _first_argmax(gscore, rowG, N_GROUPS)
        hit = rowG == f
        gmask = jnp.where(hit, 1.0, gmask)
        gscore = jnp.where(hit, neg_inf, gscore)
    emask = jnp.concatenate(
        [jnp.broadcast_to(gmask[g:g + 1, :], (gsz, tr)) for g in range(N_GROUPS)], axis=0)

    rowE = lax.broadcasted_iota(jnp.int32, (N_EXPERTS, tr), 0).astype(F32)
    masked = jnp.where(emask > 0.5, choice, neg_inf)
    hits = []
    self32 = jnp.zeros((N_EXPERTS, tr), F32)
    for _ in range(TOP_K):
        _, f = _first_argmax(masked, rowE, N_EXPERTS)
        hit = rowE == f
        hits.append(hit)
        self32 = jnp.where(hit, 1.0, self32)
        masked = jnp.where(hit, neg_inf, masked)

    wsel = self32 * s
    denom = jnp.sum(wsel, axis=0, keepdims=True)
    w = wsel / denom * ROUTED_SCALE

    rank = _dot(self32.astype(BF16), _strict_lower(tr, tr, transpose=True))
    n_e = jnp.sum(self32, axis=1, keepdims=True)
    run_len = jnp.floor((n_e + (RUN_ALIGN - 1)) * (1.0 / RUN_ALIGN)) * RUN_ALIGN
    run_len_b = jnp.broadcast_to(run_len, (N_EXPERTS, 128))
    run_start = _dot(_strict_lower(N_EXPERTS, N_EXPERTS), run_len_b.astype(BF16))[:, 0:1]
    pos = rank + run_start

    len_ref[0] = run_len_b.astype(jnp.int32)
    off_ref[0] = run_ref[...].astype(jnp.int32)
    run_ref[...] = run_ref[...] + run_len_b

    for k, hit in enumerate(hits):
        wk_ref[k:k + 1, :] = jnp.sum(jnp.where(hit, w, 0.0), axis=0, keepdims=True)
        pos_ref[k:k + 1, :] = jnp.sum(jnp.where(hit, pos, 0.0), axis=0, keepdims=True).astype(jnp.int32)


def _route(lt, rb, tr):
    e, t = lt.shape
    n_tiles = t // tr
    kblk = pl.BlockSpec((TOP_K, tr), lambda i: (0, i))
    tblk = pl.BlockSpec((1, e, 128), lambda i: (i, 0, 0))
    return pl.pallas_call(
        functools.partial(_route_kernel, tr=tr),
        grid=(n_tiles,),
        in_specs=[pl.BlockSpec((e, tr), lambda i: (0, i)),
                  pl.BlockSpec(rb.shape, lambda i: (0, 0))],
        out_specs=[kblk, kblk, tblk, tblk],
        out_shape=[jax.ShapeDtypeStruct((TOP_K, t), jnp.int32),
                   jax.ShapeDtypeStruct((TOP_K, t), F32),
                   jax.ShapeDtypeStruct((n_tiles, e, 128), jnp.int32),
                   jax.ShapeDtypeStruct((n_tiles, e, 128), jnp.int32)],
        scratch_shapes=[pltpu.VMEM((e, 128), F32)],
        compiler_params=_cparams(("arbitrary",)),
        name="route",
    )(lt, rb)


def _run_copy(vmem_buf, hbm_ref, off, row, n, sem, to_hbm):
    n = pl.multiple_of(n, RUN_ALIGN)
    v = vmem_buf.at[pl.ds(pl.multiple_of(off, RUN_ALIGN), n), :]
    h = hbm_ref.at[pl.ds(pl.multiple_of(row, RUN_ALIGN), n), :]
    return pltpu.make_async_copy(v, h, sem) if to_hbm else pltpu.make_async_copy(h, v, sem)


def _start_runs(len_ref, row_ref, tile, vmem_buf, hbm_ref, sem, *, to_hbm):
    def body(e, off):
        n = len_ref[tile * N_EXPERTS + e]

        @pl.when(n > 0)
        def _():
            _run_copy(vmem_buf, hbm_ref, off, row_ref[tile * N_EXPERTS + e], n, sem, to_hbm).start()
        return off + n

    lax.fori_loop(0, N_EXPERTS, body, jnp.int32(0))


def _wait_runs(rows_ref, tile, vmem_buf, hbm_ref, sem, *, to_hbm):
    n = rows_ref[tile]

    @pl.when(n > 0)
    def _():
        _run_copy(vmem_buf, hbm_ref, 0, 0, n, sem, to_hbm).wait()


def _assignment_rows(pos_ref, val_ref, chunk, n_rows):
    tr = pos_ref.shape[1]
    rows = lax.broadcasted_iota(jnp.int32, (n_rows, tr), 0).astype(F32).astype(BF16)
    m = jnp.zeros((n_rows, tr), BF16)
    for k in range(TOP_K):
        rel = (pos_ref[k:k + 1, :] - chunk * n_rows).astype(F32).astype(BF16)
        v = jnp.ones((1, tr), BF16) if val_ref is None else val_ref[k:k + 1, :].astype(BF16)
        m = jnp.where(rows == rel, v, m)
    return m


def _dispatch_kernel(len_ref, row_ref, rows_ref, zb_ref, pos_ref, h1_ref, xs_ref, buf, zbuf, h1b, sem, zsem, *, rs):
    tile = pl.program_id(0)
    n_tiles = pl.num_programs(0)
    slot = tile % 2

    @pl.when(tile == 0)
    def _():
        zbuf[...] = jnp.zeros(zbuf.shape, zbuf.dtype)

        def zcopy(j):
            return pltpu.make_async_copy(
                zbuf, xs_ref.at[pl.ds(zb_ref[j] * EXPERT_BLOCK, EXPERT_BLOCK), :], zsem)

        def zstart(j, carry):
            @pl.when(zb_ref[j] >= 0)
            def _():
                zcopy(j).start()
            return carry

        def zwait(j, carry):
            @pl.when(zb_ref[j] >= 0)
            def _():
                zcopy(j).wait()
            return carry

        lax.fori_loop(0, zb_ref.shape[0], zstart, 0)
        lax.fori_loop(0, zb_ref.shape[0], zwait, 0)

    @pl.when(tile >= 2)
    def _():
        _wait_runs(rows_ref, tile - 2, buf.at[slot], xs_ref, sem.at[slot], to_hbm=True)

    h1b[...] = h1_ref[...].astype(BF16)
    for c in range(rs // SORT_CHUNK):
        onehot = _assignment_rows(pos_ref, None, c, SORT_CHUNK)
        srt = _dot(onehot, h1b[...])
        buf[slot, c * SORT_CHUNK:(c + 1) * SORT_CHUNK, :] = srt.astype(BF16)
    _start_runs(len_ref, row_ref, tile, buf.at[slot], xs_ref, sem.at[slot], to_hbm=True)

    @pl.when(tile == n_tiles - 1)
    def _():
        @pl.when(tile >= 1)
        def _():
            _wait_runs(rows_ref, tile - 1, buf.at[1 - slot], xs_ref, sem.at[1 - slot], to_hbm=True)
        _wait_runs(rows_ref, tile, buf.at[slot], xs_ref, sem.at[slot], to_hbm=True)


def _dispatch(run_len, run_row, tile_rows, zero_blocks, pos, h1, n_rows, tr, rs):
    t, d = h1.shape
    return pl.pallas_call(
        functools.partial(_dispatch_kernel, rs=rs),
        grid_spec=pltpu.PrefetchScalarGridSpec(
            num_scalar_prefetch=4, grid=(t // tr,),
            in_specs=[pl.BlockSpec((TOP_K, tr), lambda i, *_: (0, i)),
                      pl.BlockSpec((tr, d), lambda i, *_: (i, 0))],
            out_specs=pl.BlockSpec(memory_space=pl.ANY),
            scratch_shapes=[pltpu.VMEM((2, rs, d), BF16),
                            pltpu.VMEM((EXPERT_BLOCK, d), BF16),
                            pltpu.VMEM((tr, d), BF16),
                            pltpu.SemaphoreType.DMA((2,)), pltpu.SemaphoreType.DMA]),
        out_shape=jax.ShapeDtypeStruct((n_rows, d), BF16),
        compiler_params=_cparams(("arbitrary",)),
        name="dispatch",
    )(run_len, run_row, tile_rows, zero_blocks, pos, h1)


def _expert_kernel(be_ref, nu_ref, xs_ref, wg_ref, wu_ref, wd_ref, ys_ref, wgu_s, wd_s, *, ed):
    i = pl.program_id(0)
    live = i < nu_ref[0]
    new_expert = jnp.logical_or(i == 0, be_ref[i] != be_ref[jnp.maximum(i - 1, 0)])

    @pl.when(jnp.logical_and(live, new_expert))
    def _():
        wgu_s[:, :ed] = wg_ref[0].astype(BF16)
        wgu_s[:, ed:] = wu_ref[0].astype(BF16)
        wd_s[...] = wd_ref[0].astype(BF16)

    @pl.when(live)
    def _():
        gu = _dot(xs_ref[...], wgu_s[...])
        g = gu[:, :ed]
        act = (g * _sigmoid(g) * gu[:, ed:]).astype(BF16)
        ys_ref[...] = _dot(act, wd_s[...]).astype(BF16)

    @pl.when(jnp.logical_not(live))
    def _():
        ys_ref[...] = jnp.zeros(ys_ref.shape, ys_ref.dtype)


def _experts(block_e, n_used, xs, wg, wu, wd):
    n_rows, d = xs.shape
    n_blocks = n_rows // EXPERT_BLOCK
    ed = wg.shape[2]

    def xmap(i, be, nu):
        return (jnp.minimum(i, nu[0] - 1), 0)

    def ymap(i, be, nu):
        return (i, 0)

    def wmap(i, be, nu):
        return (be[i], 0, 0)

    return pl.pallas_call(
        functools.partial(_expert_kernel, ed=ed),
        grid_spec=pltpu.PrefetchScalarGridSpec(
            num_scalar_prefetch=2, grid=(n_blocks,),
            in_specs=[pl.BlockSpec((EXPERT_BLOCK, d), xmap),
                      pl.BlockSpec((1, d, ed), wmap),
                      pl.BlockSpec((1, d, ed), wmap),
                      pl.BlockSpec((1, ed, d), wmap)],
            out_specs=pl.BlockSpec((EXPERT_BLOCK, d), ymap),
            scratch_shapes=[pltpu.VMEM((d, 2 * ed), BF16), pltpu.VMEM((ed, d), BF16)]),
        out_shape=jax.ShapeDtypeStruct((n_rows, d), BF16),
        compiler_params=_cparams(("arbitrary",)),
        name="experts",
    )(block_e, n_used, xs, wg, wu, wd)


def _combine_kernel(len_ref, row_ref, rows_ref, h1_ref, pos_ref, wk_ref, ys_ref, sgu_ref, sd_ref,
                    l2g_ref, l2b_ref, out_ref, ybuf, wmt, sem, *, rs, sdim):
    tile = pl.program_id(0)
    n_tiles = pl.num_programs(0)
    slot = tile % 2

    @pl.when(tile == 0)
    def _():
        ybuf[...] = jnp.zeros(ybuf.shape, ybuf.dtype)
        _start_runs(len_ref, row_ref, tile, ybuf.at[0], ys_ref, sem.at[0], to_hbm=False)

    @pl.when(tile + 1 < n_tiles)
    def _():
        _start_runs(len_ref, row_ref, tile + 1, ybuf.at[1 - slot], ys_ref, sem.at[1 - slot], to_hbm=False)

    h1 = h1_ref[...]
    gu = _dot(h1.astype(BF16), sgu_ref[...])
    g = gu[:, :sdim]
    act = (g * _sigmoid(g) * gu[:, sdim:]).astype(BF16)
    acc = DN_ALPHA * h1 + _dot(act, sd_ref[...])

    for c in range(rs // SORT_CHUNK):
        wmt[c * SORT_CHUNK:(c + 1) * SORT_CHUNK, :] = _assignment_rows(pos_ref, wk_ref, c, SORT_CHUNK)

    _wait_runs(rows_ref, tile, ybuf.at[slot], ys_ref, sem.at[slot], to_hbm=False)
    acc = acc + lax.dot_general(wmt[...], ybuf[slot], (((0,), (0,)), ((), ())), preferred_element_type=F32)
    out_ref[...] = _ln(acc, l2g_ref[...], l2b_ref[...])


def _combine(run_len, run_row, tile_rows, h1, pos, wk, ys, sgu, sd, l2g, l2b, tr, rs):
    t, d = h1.shape
    sdim = sd.shape[0]
    row = pl.BlockSpec((tr, d), lambda i, *_: (i, 0))
    krow = pl.BlockSpec((TOP_K, tr), lambda i, *_: (0, i))
    vec = pl.BlockSpec((1, d), lambda i, *_: (0, 0))
    return pl.pallas_call(
        functools.partial(_combine_kernel, rs=rs, sdim=sdim),
        grid_spec=pltpu.PrefetchScalarGridSpec(
            num_scalar_prefetch=3, grid=(t // tr,),
            in_specs=[row, krow, krow,
                      pl.BlockSpec(memory_space=pl.ANY),
                      pl.BlockSpec(sgu.shape, lambda i, *_: (0, 0)),
                      pl.BlockSpec(sd.shape, lambda i, *_: (0, 0)),
                      vec, vec],
            out_specs=row,
            scratch_shapes=[pltpu.VMEM((2, rs, d), BF16),
                            pltpu.VMEM((rs, tr), BF16),
                            pltpu.SemaphoreType.DMA((2,))]),
        out_shape=jax.ShapeDtypeStruct((t, d), F32),
        compiler_params=_cparams(("arbitrary",)),
        name="combine",
    )(run_len, run_row, tile_rows, h1, pos, wk, ys, sgu, sd, l2g, l2b)


def _tile(n, pref):
    while n % pref:
        pref //= 2
    return pref


def kernel(x, mem, ln_in_g, ln_in_b, w_in, gmlp_norm_g, gmlp_norm_b, gmlp_spatial_w, gmlp_spatial_b, conv_w, conv_b, lru_wa, lru_ba, lru_wi, lru_bi, lru_lambda, w_kv, w_br_gmlp, w_br_lru, w_br_xa, w_out, ln1_g, ln1_b, w_router, router_bias, exp_gate, exp_up, exp_down, sh_gate, sh_up, sh_down, ln2_g, ln2_b):
    b, s, d = x.shape
    t = b * s
    assert w_in.shape[0] == DEPTH and w_in.shape[2] == 8 * d and s % GMLP_CHUNK == 0
    r2 = lambda a: a.reshape(1, -1)
    x2 = x.reshape(t, d)
    lg, lb = r2(ln_in_g), r2(ln_in_b)
    hd = d // XA_HEADS

    tm = _tile(t, 512)
    u, v, lx, gl, q, g0, g1, g2 = _inproj(
        x2, lg, lb, w_in[0].astype(BF16), r2(gmlp_norm_g[0]), r2(gmlp_norm_b[0]), tm, hd ** -0.5)

    wkv = w_kv[0]
    kt, vv = _kv(mem, wkv[:, :d].T.astype(BF16), wkv[:, d:].astype(BF16))

    yg = _gmlp(u, v, gmlp_spatial_w[0], gmlp_spatial_b[0].T, _tile(s, 512))

    wai = jnp.concatenate([lru_wa[0], lru_wi[0]], axis=-1).astype(BF16)
    yl = _lru(lx, gl, conv_w[0], r2(conv_b[0]), wai, r2(lru_ba[0]), r2(lru_bi[0]), r2(lru_lambda[0]),
              b, _tile(s, 256))

    yx = _xattn(q, kt, vv, b, _tile(s, 512))

    h1, lt = _merge(x2, lg, lb, yg, yl, yx, g0, g1, g2,
                    w_br_gmlp[0].astype(BF16), w_br_lru[0].astype(BF16), w_br_xa[0].astype(BF16),
                    w_out[0].astype(BF16), r2(ln1_g[0]), r2(ln1_b[0]), w_router[0].T, tm)

    tr = ROUTE_TILE
    assert t % tr == 0
    n_tiles = t // tr
    rs = tr * TOP_K + N_EXPERTS * RUN_ALIGN
    rb = jnp.broadcast_to(router_bias[0].astype(F32)[:, None], (N_EXPERTS, 128))
    pos, wk, run_len3, run_off3 = _route(lt, rb, tr)

    run_len = run_len3[:, :, 0]
    run_off = run_off3[:, :, 0]
    total = run_off[-1] + run_len[-1]
    seg_len = (total + EXPERT_BLOCK - 1) // EXPERT_BLOCK * EXPERT_BLOCK
    seg_end = jnp.cumsum(seg_len)
    seg_start = seg_end - seg_len
    run_row = seg_start[None, :] + run_off
    tile_rows = jnp.sum(run_len, axis=1).astype(jnp.int32)
    max_rows = t * TOP_K + n_tiles * N_EXPERTS * (RUN_ALIGN - 1)
    n_blocks = -(-max_rows // EXPERT_BLOCK) + N_EXPERTS
    block_start = jnp.arange(n_blocks, dtype=jnp.int32) * EXPERT_BLOCK
    block_e = jnp.minimum(jnp.sum(block_start[:, None] >= seg_end[None, :], axis=1),
                          N_EXPERTS - 1).astype(jnp.int32)
    n_used = (seg_end[-1:] // EXPERT_BLOCK).astype(jnp.int32)
    n_tail = n_blocks - (t * TOP_K) // EXPERT_BLOCK
    tail_blk = n_used[0] + jnp.arange(n_tail, dtype=jnp.int32)
    zero_blocks = jnp.concatenate([
        jnp.where(total % EXPERT_BLOCK != 0, seg_end // EXPERT_BLOCK - 1, -1),
        jnp.where(tail_blk < n_blocks, tail_blk, -1)]).astype(jnp.int32)

    run_len_flat = run_len.reshape(-1).astype(jnp.int32)
    run_row_flat = run_row.reshape(-1).astype(jnp.int32)
    xs = _dispatch(run_len_flat, run_row_flat, tile_rows, zero_blocks, pos, h1,
                   n_blocks * EXPERT_BLOCK, tr, rs)
    ys = _experts(block_e, n_used, xs, exp_gate[0], exp_up[0], exp_down[0])
    sgu = jnp.concatenate([sh_gate[0], sh_up[0]], axis=-1).astype(BF16)
    out = _combine(run_len_flat, run_row_flat, tile_rows, h1, pos, wk, ys, sgu, sh_down[0].astype(BF16),
                   r2(ln2_g[0]), r2(ln2_b[0]), tr, rs)
    return out.reshape(b, s, d)
```

```python
import functools

import jax
import jax.numpy as jnp
from jax import lax
from jax.experimental import pallas as pl
from jax.experimental.pallas import tpu as pltpu

F32 = jnp.float32
BF16 = jnp.bfloat16

GMLP_GROUPS = 8
GMLP_CHUNK = 128
LRU_BLOCKS = 8
CONV_WIDTH = 4
LRU_C = 8.0
XA_HEADS = 4
N_EXPERTS = 64
TOP_K = 8
N_GROUPS = 8
TOPK_GROUPS = 4
ROUTED_SCALE = 2.5
EXPERT_BLOCK = 1024
ROUTE_TILE = 256
RUN_ALIGN = 16
SORT_CHUNK = 256
SCAN_GROUP = 8
LRU_ROWS = 128
LN_EPS = 1e-5
DEPTH = 1
DN_ALPHA = (2.0 * DEPTH) ** 0.25

VMEM_LIMIT_V7X = 56 * 1024 * 1024


def _cparams(sem):
    return pltpu.CompilerParams(dimension_semantics=sem, vmem_limit_bytes=VMEM_LIMIT_V7X)


def _ln(x, g, b):
    mu = jnp.mean(x, axis=-1, keepdims=True)
    xc = x - mu
    var = jnp.mean(xc * xc, axis=-1, keepdims=True)
    return xc * lax.rsqrt(var + LN_EPS) * g + b


def _gelu(x):
    return x * (0.5 * (1.0 + jnp.tanh(0.7978845608028654 * (x + 0.044715 * (x * x * x)))))


def _sigmoid(x):
    return 0.5 * (1.0 + jnp.tanh(0.5 * x))


def _dot(a, b):
    return jnp.dot(a, b, preferred_element_type=F32)


def _lru_block(blk, r0, xe_ref, gate_ref, cw_ref, cb_ref, wai_ref, ba_ref, bi_ref, lam_ref, hc_ref, y_ref,
               *, nr, bd):
    cs = slice(blk * bd, (blk + 1) * bd)
    xc = cb_ref[:, cs] + jnp.zeros((nr, bd), F32)
    for k in range(CONV_WIDTH):
        xc = xc + cw_ref[k:k + 1, cs] * xe_ref[8 + r0 - k:8 + r0 - k + nr, cs]
    ri = _dot(xc.astype(BF16), wai_ref[blk])
    r = _sigmoid(ri[:, :bd] + ba_ref[:, cs])
    ig = _sigmoid(ri[:, bd:] + bi_ref[:, cs])
    lam = lam_ref[:, cs]
    softplus_neg = jnp.maximum(-lam, 0.0) + jnp.log1p(jnp.exp(-jnp.abs(lam)))
    log_a = (-LRU_C) * r * softplus_neg
    a = jnp.exp(log_a)
    mult = jnp.sqrt(1.0 - a * a)
    b = mult * (ig * xc)
    row_in_group = lax.broadcasted_iota(jnp.int32, (nr, bd), 0) % SCAN_GROUP
    dsh = 1
    while dsh < SCAN_GROUP:
        a_sh = pltpu.roll(a, dsh, 0)
        b_sh = pltpu.roll(b, dsh, 0)
        keep = row_in_group >= dsh
        b = jnp.where(keep, a * b_sh + b, b)
        a = jnp.where(keep, a * a_sh, a)
        dsh *= 2
    carry = hc_ref[0:1, cs]
    hs = []
    for g in range(nr // SCAN_GROUP):
        rows = slice(g * SCAN_GROUP, (g + 1) * SCAN_GROUP)
        hs.append(b[rows, :] + a[rows, :] * carry)
        carry = hs[-1][SCAN_GROUP - 1:SCAN_GROUP, :]
    hc_ref[0:1, cs] = carry
    y_ref[r0:r0 + nr, cs] = (jnp.concatenate(hs, axis=0) * gate_ref[r0:r0 + nr, cs]).astype(BF16)


def _inproj_kernel(x_ref, lg_ref, lb_ref, w_ref, ng_ref, nb_ref, cw_ref, cb_ref, wai_ref, ba_ref, bi_ref, lam_ref,
                   u_ref, v_ref, q_ref, g0_ref, g1_ref, g2_ref, yl_ref, xe_ref, hc_ref, gate_ref,
                   *, d, q_scale):
    @pl.when(pl.program_id(1) == 0)
    def _():
        xe_ref[0:8, :] = jnp.zeros((8, d), F32)
        hc_ref[...] = jnp.zeros(hc_ref.shape, F32)

    tm = x_ref.shape[0]
    hb = _ln(x_ref[...], lg_ref[...], lb_ref[...]).astype(BF16)

    def z(j):
        return _dot(hb, w_ref[:, j * d:(j + 1) * d])

    def lru(blk):
        for r0 in range(0, tm, LRU_ROWS):
            _lru_block(blk, r0, xe_ref, gate_ref, cw_ref, cb_ref, wai_ref, ba_ref, bi_ref, lam_ref, hc_ref,
                       yl_ref, nr=LRU_ROWS, bd=d // LRU_BLOCKS)

    xe_ref[8:8 + tm, :] = z(2)
    gate_ref[...] = _gelu(z(3))
    u_ref[...] = _gelu(z(0)).astype(BF16)
    lru(0), lru(1)
    v_ref[...] = _ln(_gelu(z(1)), ng_ref[...], nb_ref[...]).astype(BF16)
    lru(2), lru(3)
    q_ref[...] = (z(4) * q_scale).astype(BF16)
    lru(4)
    g0_ref[...] = _sigmoid(z(5)).astype(BF16)
    lru(5)
    g1_ref[...] = _sigmoid(z(6)).astype(BF16)
    lru(6)
    g2_ref[...] = _sigmoid(z(7)).astype(BF16)
    lru(7)
    xe_ref[0:8, :] = xe_ref[tm:tm + 8, :]


def _inproj(x2, ln_g, ln_b, w_in, ng, nb, conv_w, conv_b, wai, ba, bi, lam, batch, tm, q_scale):
    t, d = x2.shape
    n_s = t // batch // tm
    n_out = 7
    row = pl.BlockSpec((tm, d), lambda b, s: (b * n_s + s, 0))
    vec = pl.BlockSpec((1, d), lambda b, s: (0, 0))
    wspec = pl.BlockSpec(w_in.shape, lambda b, s: (0, 0), pipeline_mode=pl.Buffered(1))
    return pl.pallas_call(
        functools.partial(_inproj_kernel, d=d, q_scale=q_scale),
        grid=(batch, n_s),
        in_specs=[row, vec, vec, wspec, vec, vec,
                  pl.BlockSpec(conv_w.shape, lambda b, s: (0, 0)), vec,
                  pl.BlockSpec(wai.shape, lambda b, s: (0, 0, 0)), vec, vec, vec],
        out_specs=[row] * n_out,
        out_shape=[jax.ShapeDtypeStruct((t, d), BF16)] * n_out,
        scratch_shapes=[pltpu.VMEM((tm + 8, d), F32), pltpu.VMEM((8, d), F32), pltpu.VMEM((tm, d), F32)],
        compiler_params=_cparams(("arbitrary", "arbitrary")),
        name="inproj",
    )(x2, ln_g, ln_b, w_in, ng, nb, conv_w, conv_b, wai, ba, bi, lam)


def _kv_kernel(mem_ref, wkt_ref, wv_ref, kt_ref, v_ref):
    m = mem_ref[0].astype(BF16)
    kt = lax.dot_general(wkt_ref[...], m, (((1,), (1,)), ((), ())), preferred_element_type=F32)
    kt_ref[0] = kt.astype(BF16)
    v_ref[0] = _dot(m, wv_ref[...]).astype(BF16)


def _kv(mem, wkt, wv):
    b, m, d = mem.shape
    xw = wv.shape[1]
    return pl.pallas_call(
        _kv_kernel,
        grid=(b,),
        in_specs=[pl.BlockSpec((1, m, d), lambda i: (i, 0, 0)),
                  pl.BlockSpec(wkt.shape, lambda i: (0, 0)),
                  pl.BlockSpec(wv.shape, lambda i: (0, 0))],
        out_specs=[pl.BlockSpec((1, xw, m), lambda i: (i, 0, 0)),
                   pl.BlockSpec((1, m, xw), lambda i: (i, 0, 0))],
        out_shape=[jax.ShapeDtypeStruct((b, xw, m), BF16), jax.ShapeDtypeStruct((b, m, xw), BF16)],
        compiler_params=_cparams(("parallel",)),
        name="kv",
    )(mem, wkt, wv)


def _gmlp_kernel(u_ref, v_ref, w_ref, bt_ref, y_ref, *, n_chunks):
    c = GMLP_CHUNK
    rows = lax.broadcasted_iota(jnp.int32, (c, c), 0)
    cols = lax.broadcasted_iota(jnp.int32, (c, c), 1)
    causal = cols <= rows
    for g in range(GMLP_GROUPS):
        wg = jnp.where(causal, w_ref[g], 0.0).astype(BF16)
        bcol = bt_ref[:, g:g + 1]
        for n in range(n_chunks):
            vb = v_ref[n * c:(n + 1) * c, g * c:(g + 1) * c]
            mixed = _dot(wg, vb) + bcol
            ub = u_ref[n * c:(n + 1) * c, g * c:(g + 1) * c].astype(F32)
            y_ref[n * c:(n + 1) * c, g * c:(g + 1) * c] = (ub * mixed).astype(BF16)


def _gmlp(u, v, w_sp, b_sp_t, ts):
    t, d = u.shape
    row = pl.BlockSpec((ts, d), lambda i: (i, 0))
    return pl.pallas_call(
        functools.partial(_gmlp_kernel, n_chunks=ts // GMLP_CHUNK),
        grid=(t // ts,),
        in_specs=[row, row,
                  pl.BlockSpec(w_sp.shape, lambda i: (0, 0, 0)),
                  pl.BlockSpec(b_sp_t.shape, lambda i: (0, 0))],
        out_specs=row,
        out_shape=jax.ShapeDtypeStruct((t, d), BF16),
        compiler_params=_cparams(("parallel",)),
        name="gmlp",
    )(u, v, w_sp, b_sp_t)


def _xattn_kernel(q_ref, kt_ref, v_ref, y_ref, *, hd):
    for h in range(XA_HEADS):
        cs = slice(h * hd, (h + 1) * hd)
        sc = _dot(q_ref[:, cs], kt_ref[0, cs, :])
        m = jnp.max(sc, axis=-1, keepdims=True)
        e = jnp.exp(sc - m)
        l = jnp.sum(e, axis=-1, keepdims=True)
        o = _dot(e.astype(BF16), v_ref[0, :, cs])
        y_ref[:, cs] = (o / l).astype(BF16)


def _xattn(q, kt, v, batch, ts):
    t, d = q.shape
    n_s = t // batch // ts
    row = pl.BlockSpec((ts, d), lambda b, s: (b * n_s + s, 0))
    return pl.pallas_call(
        functools.partial(_xattn_kernel, hd=d // XA_HEADS),
        grid=(batch, n_s),
        in_specs=[row,
                  pl.BlockSpec((1,) + kt.shape[1:], lambda b, s: (b, 0, 0)),
                  pl.BlockSpec((1,) + v.shape[1:], lambda b, s: (b, 0, 0))],
        out_specs=row,
        out_shape=jax.ShapeDtypeStruct((t, d), BF16),
        compiler_params=_cparams(("parallel", "parallel")),
        name="xattn",
    )(q, kt, v)


def _merge_kernel(x_ref, lg_ref, lb_ref, yg_ref, yl_ref, yx_ref, g0_ref, g1_ref, g2_ref,
                  w0_ref, w1_ref, w2_ref, wo_ref, l1g_ref, l1b_ref, wrt_ref, h1_ref, lt_ref):
    h0 = _ln(x_ref[...], lg_ref[...], lb_ref[...])
    merged = (g0_ref[...].astype(F32) * _dot(yg_ref[...], w0_ref[...])
              + g1_ref[...].astype(F32) * _dot(yl_ref[...], w1_ref[...])
              + g2_ref[...].astype(F32) * _dot(yx_ref[...], w2_ref[...]))
    h1 = _ln(DN_ALPHA * h0 + _dot(merged.astype(BF16), wo_ref[...]), l1g_ref[...], l1b_ref[...])
    h1_ref[...] = h1
    lt_ref[...] = lax.dot_general(wrt_ref[...], h1, (((1,), (1,)), ((), ())),
                                  precision=lax.Precision.HIGHEST, preferred_element_type=F32)


def _merge(x2, ln_g, ln_b, yg, yl, yx, g0, g1, g2, w0, w1, w2, wo, l1g, l1b, wrt, tm):
    t, d = x2.shape
    e = wrt.shape[0]
    row = pl.BlockSpec((tm, d), lambda i: (i, 0))
    vec = pl.BlockSpec((1, d), lambda i: (0, 0))
    wsq = pl.BlockSpec((d, d), lambda i: (0, 0))
    return pl.pallas_call(
        _merge_kernel,
        grid=(t // tm,),
        in_specs=[row, vec, vec, row, row, row, row, row, row, wsq, wsq, wsq, wsq, vec, vec,
                  pl.BlockSpec(wrt.shape, lambda i: (0, 0))],
        out_specs=[row, pl.BlockSpec((e, tm), lambda i: (0, i))],
        out_shape=[jax.ShapeDtypeStruct((t, d), F32), jax.ShapeDtypeStruct((e, t), F32)],
        compiler_params=_cparams(("parallel",)),
        name="merge",
    )(x2, ln_g, ln_b, yg, yl, yx, g0, g1, g2, w0, w1, w2, wo, l1g, l1b, wrt)


def _first_argmax(x, rowf, n):
    m = jnp.max(x, axis=0, keepdims=True)
    first = jnp.min(jnp.where(x == m, rowf, float(n)), axis=0, keepdims=True)
    return m, first


def _strict_lower(n, m, transpose=False):
    r = lax.broadcasted_iota(jnp.int32, (n, m), 0)
    c = lax.broadcasted_iota(jnp.int32, (n, m), 1)
    return jnp.where((r < c) if transpose else (c < r), 1.0, 0.0).astype(BF16)


def _route_kernel(lt_ref, rb_ref, pos_ref, wk_ref, len_ref, off_ref, run_ref, *, tr):
    @pl.when(pl.program_id(0) == 0)
    def _():
        run_ref[...] = jnp.zeros(run_ref.shape, F32)

    gsz = N_EXPERTS // N_GROUPS
    s = _sigmoid(lt_ref[...])
    choice = s + rb_ref[:, 0:1]
    neg_inf = -jnp.inf

    rowg = lax.broadcasted_iota(jnp.int32, (gsz, tr), 0).astype(F32)
    gs_rows = []
    for g in range(N_GROUPS):
        cg = choice[g * gsz:(g + 1) * gsz, :]
        m1, f1 = _first_argmax(cg, rowg, gsz)
        m2 = jnp.max(jnp.where(rowg == f1, neg_inf, cg), axis=0, keepdims=True)
        gs_rows.append(m1 + m2)
    gscore = jnp.concatenate(gs_rows, axis=0)

    rowG = lax.broadcasted_iota(jnp.int32, (N_GROUPS, tr), 0).astype(F32)
    gmask = jnp.zeros((N_GROUPS, tr), F32)
    for _ in range(TOPK_GROUPS):
        _, f = _first_argmax(gscore, rowG, N_GROUPS)
        hit = rowG == f
        gmask = jnp.where(hit, 1.0, gmask)
        gscore = jnp.where(hit, neg_inf, gscore)
    emask = jnp.concatenate(
        [jnp.broadcast_to(gmask[g:g + 1, :], (gsz, tr)) for g in range(N_GROUPS)], axis=0)

    rowE = lax.broadcasted_iota(jnp.int32, (N_EXPERTS, tr), 0).astype(F32)
    masked = jnp.where(emask > 0.5, choice, neg_inf)
    hits = []
    self32 = jnp.zeros((N_EXPERTS, tr), F32)
    for _ in range(TOP_K):
        _, f = _first_argmax(masked, rowE, N_EXPERTS)
        hit = rowE == f
        hits.append(hit)
        self32 = jnp.where(hit, 1.0, self32)
        masked = jnp.where(hit, neg_inf, masked)

    wsel = self32 * s
    denom = jnp.sum(wsel, axis=0, keepdims=True)
    w = wsel / denom * ROUTED_SCALE

    rank = _dot(self32.astype(BF16), _strict_lower(tr, tr, transpose=True))
    n_e = jnp.sum(self32, axis=1, keepdims=True)
    run_len = jnp.floor((n_e + (RUN_ALIGN - 1)) * (1.0 / RUN_ALIGN)) * RUN_ALIGN
    run_len_b = jnp.broadcast_to(run_len, (N_EXPERTS, 128))
    run_start = _dot(_strict_lower(N_EXPERTS, N_EXPERTS), run_len_b.astype(BF16))[:, 0:1]
    pos = rank + run_start

    len_ref[0] = run_len_b.astype(jnp.int32)
    off_ref[0] = run_ref[...].astype(jnp.int32)
    run_ref[...] = run_ref[...] + run_len_b

    for k, hit in enumerate(hits):
        wk_ref[k:k + 1, :] = jnp.sum(jnp.where(hit, w, 0.0), axis=0, keepdims=True)
        pos_ref[k:k + 1, :] = jnp.sum(jnp.where(hit, pos, 0.0), axis=0, keepdims=True).astype(jnp.int32)


def _route(lt, rb, tr):
    e, t = lt.shape
    n_tiles = t // tr
    kblk = pl.BlockSpec((TOP_K, tr), lambda i: (0, i))
    tblk = pl.BlockSpec((1, e, 128), lambda i: (i, 0, 0))
    return pl.pallas_call(
        functools.partial(_route_kernel, tr=tr),
        grid=(n_tiles,),
        in_specs=[pl.BlockSpec((e, tr), lambda i: (0, i)),
                  pl.BlockSpec(rb.shape, lambda i: (0, 0))],
        out_specs=[kblk, kblk, tblk, tblk],
        out_shape=[jax.ShapeDtypeStruct((TOP_K, t), jnp.int32),
                   jax.ShapeDtypeStruct((TOP_K, t), F32),
                   jax.ShapeDtypeStruct((n_tiles, e, 128), jnp.int32),
                   jax.ShapeDtypeStruct((n_tiles, e, 128), jnp.int32)],
        scratch_shapes=[pltpu.VMEM((e, 128), F32)],
        compiler_params=_cparams(("arbitrary",)),
        name="route",
    )(lt, rb)


def _run_copy(vmem_buf, hbm_ref, off, row, n, sem, to_hbm):
    n = pl.multiple_of(n, RUN_ALIGN)
    v = vmem_buf.at[pl.ds(pl.multiple_of(off, RUN_ALIGN), n), :]
    h = hbm_ref.at[pl.ds(pl.multiple_of(row, RUN_ALIGN), n), :]
    return pltpu.make_async_copy(v, h, sem) if to_hbm else pltpu.make_async_copy(h, v, sem)


def _start_runs(len_ref, row_ref, tile, vmem_buf, hbm_ref, sem, *, to_hbm):
    def body(e, off):
        n = len_ref[tile * N_EXPERTS + e]

        @pl.when(n > 0)
        def _():
            _run_copy(vmem_buf, hbm_ref, off, row_ref[tile * N_EXPERTS + e], n, sem, to_hbm).start()
        return off + n

    lax.fori_loop(0, N_EXPERTS, body, jnp.int32(0))


def _wait_runs(rows_ref, tile, vmem_buf, hbm_ref, sem, *, to_hbm):
    n = rows_ref[tile]

    @pl.when(n > 0)
    def _():
        _run_copy(vmem_buf, hbm_ref, 0, 0, n, sem, to_hbm).wait()


def _chunks_always(tr):
    return (tr * TOP_K + N_EXPERTS * RUN_ALIGN // 2) // SORT_CHUNK


def _assignment_rows(pos_ref, val_ref, chunk, n_rows):
    tr = pos_ref.shape[1]
    rows = lax.broadcasted_iota(jnp.int32, (n_rows, tr), 0).astype(F32).astype(BF16)
    m = jnp.zeros((n_rows, tr), BF16)
    for k in range(TOP_K):
        rel = (pos_ref[k:k + 1, :] - chunk * n_rows).astype(F32).astype(BF16)
        v = jnp.ones((1, tr), BF16) if val_ref is None else val_ref[k:k + 1, :].astype(BF16)
        m = jnp.where(rows == rel, v, m)
    return m


def _dispatch_kernel(len_ref, row_ref, rows_ref, zb_ref, pos_ref, h1_ref, xs_ref, buf, zbuf, h1b, sem, zsem, *, rs):
    tile = pl.program_id(0)
    n_tiles = pl.num_programs(0)
    slot = tile % 2

    @pl.when(tile == 0)
    def _():
        zbuf[...] = jnp.zeros(zbuf.shape, zbuf.dtype)

        def zcopy(j):
            return pltpu.make_async_copy(
                zbuf, xs_ref.at[pl.ds(zb_ref[j] * EXPERT_BLOCK, EXPERT_BLOCK), :], zsem)

        def zstart(j, carry):
            @pl.when(zb_ref[j] >= 0)
            def _():
                zcopy(j).start()
            return carry

        def zwait(j, carry):
            @pl.when(zb_ref[j] >= 0)
            def _():
                zcopy(j).wait()
            return carry

        lax.fori_loop(0, zb_ref.shape[0], zstart, 0)
        lax.fori_loop(0, zb_ref.shape[0], zwait, 0)

    @pl.when(tile >= 2)
    def _():
        _wait_runs(rows_ref, tile - 2, buf.at[slot], xs_ref, sem.at[slot], to_hbm=True)

    h1b[...] = h1_ref[...].astype(BF16)

    def sort_chunk(c):
        onehot = _assignment_rows(pos_ref, None, c, SORT_CHUNK)
        srt = _dot(onehot, h1b[...])
        buf[slot, c * SORT_CHUNK:(c + 1) * SORT_CHUNK, :] = srt.astype(BF16)

    n_always = _chunks_always(h1_ref.shape[0])
    for c in range(n_always):
        sort_chunk(c)
    for c in range(n_always, rs // SORT_CHUNK):
        pl.when(rows_ref[tile] > c * SORT_CHUNK)(functools.partial(sort_chunk, c))
    _start_runs(len_ref, row_ref, tile, buf.at[slot], xs_ref, sem.at[slot], to_hbm=True)

    @pl.when(tile == n_tiles - 1)
    def _():
        @pl.when(tile >= 1)
        def _():
            _wait_runs(rows_ref, tile - 1, buf.at[1 - slot], xs_ref, sem.at[1 - slot], to_hbm=True)
        _wait_runs(rows_ref, tile, buf.at[slot], xs_ref, sem.at[slot], to_hbm=True)


def _dispatch(run_len, run_row, tile_rows, zero_blocks, pos, h1, n_rows, tr, rs):
    t, d = h1.shape
    return pl.pallas_call(
        functools.partial(_dispatch_kernel, rs=rs),
        grid_spec=pltpu.PrefetchScalarGridSpec(
            num_scalar_prefetch=4, grid=(t // tr,),
            in_specs=[pl.BlockSpec((TOP_K, tr), lambda i, *_: (0, i)),
                      pl.BlockSpec((tr, d), lambda i, *_: (i, 0))],
            out_specs=pl.BlockSpec(memory_space=pl.ANY),
            scratch_shapes=[pltpu.VMEM((2, rs, d), BF16),
                            pltpu.VMEM((EXPERT_BLOCK, d), BF16),
                            pltpu.VMEM((tr, d), BF16),
                            pltpu.SemaphoreType.DMA((2,)), pltpu.SemaphoreType.DMA]),
        out_shape=jax.ShapeDtypeStruct((n_rows, d), BF16),
        compiler_params=_cparams(("arbitrary",)),
        name="dispatch",
    )(run_len, run_row, tile_rows, zero_blocks, pos, h1)


def _expert_kernel(be_ref, nu_ref, xs_ref, wg_ref, wu_ref, wd_ref, ys_ref, wgu_s, wd_s, *, ed):
    i = pl.program_id(0)
    live = i < nu_ref[0]
    new_expert = jnp.logical_or(i == 0, be_ref[i] != be_ref[jnp.maximum(i - 1, 0)])

    @pl.when(jnp.logical_and(live, new_expert))
    def _():
        wgu_s[:, :ed] = wg_ref[0].astype(BF16)
        wgu_s[:, ed:] = wu_ref[0].astype(BF16)
        wd_s[...] = wd_ref[0].astype(BF16)

    @pl.when(live)
    def _():
        gu = _dot(xs_ref[...], wgu_s[...])
        g = gu[:, :ed]
        act = (g * _sigmoid(g) * gu[:, ed:]).astype(BF16)
        ys_ref[...] = _dot(act, wd_s[...]).astype(BF16)

    @pl.when(jnp.logical_not(live))
    def _():
        ys_ref[...] = jnp.zeros(ys_ref.shape, ys_ref.dtype)


def _experts(block_e, n_used, xs, wg, wu, wd):
    n_rows, d = xs.shape
    n_blocks = n_rows // EXPERT_BLOCK
    ed = wg.shape[2]

    def xmap(i, be, nu):
        return (jnp.minimum(i, nu[0] - 1), 0)

    def ymap(i, be, nu):
        return (i, 0)

    def wmap(i, be, nu):
        return (be[i], 0, 0)

    return pl.pallas_call(
        functools.partial(_expert_kernel, ed=ed),
        grid_spec=pltpu.PrefetchScalarGridSpec(
            num_scalar_prefetch=2, grid=(n_blocks,),
            in_specs=[pl.BlockSpec((EXPERT_BLOCK, d), xmap),
                      pl.BlockSpec((1, d, ed), wmap),
                      pl.BlockSpec((1, d, ed), wmap),
                      pl.BlockSpec((1, ed, d), wmap)],
            out_specs=pl.BlockSpec((EXPERT_BLOCK, d), ymap),
            scratch_shapes=[pltpu.VMEM((d, 2 * ed), BF16), pltpu.VMEM((ed, d), BF16)]),
        out_shape=jax.ShapeDtypeStruct((n_rows, d), BF16),
        compiler_params=_cparams(("arbitrary",)),
        name="experts",
    )(block_e, n_used, xs, wg, wu, wd)


def _combine_kernel(len_ref, row_ref, rows_ref, h1_ref, pos_ref, wk_ref, ys_ref, sgu_ref, sd_ref,
                    l2g_ref, l2b_ref, out_ref, ybuf, wmt, sem, *, rs, sdim):
    tile = pl.program_id(0)
    n_tiles = pl.num_programs(0)
    slot = tile % 2

    @pl.when(tile == 0)
    def _():
        ybuf[...] = jnp.zeros(ybuf.shape, ybuf.dtype)
        _start_runs(len_ref, row_ref, tile, ybuf.at[0], ys_ref, sem.at[0], to_hbm=False)

    @pl.when(tile + 1 < n_tiles)
    def _():
        _start_runs(len_ref, row_ref, tile + 1, ybuf.at[1 - slot], ys_ref, sem.at[1 - slot], to_hbm=False)

    _wait_runs(rows_ref, tile, ybuf.at[slot], ys_ref, sem.at[slot], to_hbm=False)

    h1 = h1_ref[...]
    gu = _dot(h1.astype(BF16), sgu_ref[...])
    g = gu[:, :sdim]
    act = (g * _sigmoid(g) * gu[:, sdim:]).astype(BF16)
    acc = DN_ALPHA * h1 + _dot(act, sd_ref[...])

    def weights_chunk(c):
        wmt[c * SORT_CHUNK:(c + 1) * SORT_CHUNK, :] = _assignment_rows(pos_ref, wk_ref, c, SORT_CHUNK)

    def unsort(lo, hi):
        return lax.dot_general(wmt[lo:hi, :], ybuf[slot, lo:hi, :], (((0,), (0,)), ((), ())),
                               preferred_element_type=F32)

    n_always = _chunks_always(h1.shape[0])
    for c in range(n_always):
        weights_chunk(c)
    out_ref[...] = acc + unsort(0, n_always * SORT_CHUNK)
    for c in range(n_always, rs // SORT_CHUNK):
        @pl.when(rows_ref[tile] > c * SORT_CHUNK)
        def _(c=c):
            weights_chunk(c)
            out_ref[...] += unsort(c * SORT_CHUNK, (c + 1) * SORT_CHUNK)
    out_ref[...] = _ln(out_ref[...], l2g_ref[...], l2b_ref[...])


def _combine(run_len, run_row, tile_rows, h1, pos, wk, ys, sgu, sd, l2g, l2b, tr, rs):
    t, d = h1.shape
    sdim = sd.shape[0]
    row = pl.BlockSpec((tr, d), lambda i, *_: (i, 0))
    krow = pl.BlockSpec((TOP_K, tr), lambda i, *_: (0, i))
    vec = pl.BlockSpec((1, d), lambda i, *_: (0, 0))
    return pl.pallas_call(
        functools.partial(_combine_kernel, rs=rs, sdim=sdim),
        grid_spec=pltpu.PrefetchScalarGridSpec(
            num_scalar_prefetch=3, grid=(t // tr,),
            in_specs=[row, krow, krow,
                      pl.BlockSpec(memory_space=pl.ANY),
                      pl.BlockSpec(sgu.shape, lambda i, *_: (0, 0)),
                      pl.BlockSpec(sd.shape, lambda i, *_: (0, 0)),
                      vec, vec],
            out_specs=row,
            scratch_shapes=[pltpu.VMEM((2, rs, d), BF16),
                            pltpu.VMEM((rs, tr), BF16),
                            pltpu.SemaphoreType.DMA((2,))]),
        out_shape=jax.ShapeDtypeStruct((t, d), F32),
        compiler_params=_cparams(("arbitrary",)),
        name="combine",
    )(run_len, run_row, tile_rows, h1, pos, wk, ys, sgu, sd, l2g, l2b)


def _tile(n, pref):
    while n % pref:
        pref //= 2
    return pref


def kernel(x, mem, ln_in_g, ln_in_b, w_in, gmlp_norm_g, gmlp_norm_b, gmlp_spatial_w, gmlp_spatial_b, conv_w, conv_b, lru_wa, lru_ba, lru_wi, lru_bi, lru_lambda, w_kv, w_br_gmlp, w_br_lru, w_br_xa, w_out, ln1_g, ln1_b, w_router, router_bias, exp_gate, exp_up, exp_down, sh_gate, sh_up, sh_down, ln2_g, ln2_b):
    b, s, d = x.shape
    t = b * s
    assert w_in.shape[0] == DEPTH and w_in.shape[2] == 8 * d and s % GMLP_CHUNK == 0
    r2 = lambda a: a.reshape(1, -1)
    x2 = x.reshape(t, d)
    lg, lb = r2(ln_in_g), r2(ln_in_b)
    hd = d // XA_HEADS

    tm = _tile(s, 512)
    wai = jnp.concatenate([lru_wa[0], lru_wi[0]], axis=-1).astype(BF16)
    u, v, q, g0, g1, g2, yl = _inproj(
        x2, lg, lb, w_in[0].astype(BF16), r2(gmlp_norm_g[0]), r2(gmlp_norm_b[0]),
        conv_w[0], r2(conv_b[0]), wai, r2(lru_ba[0]), r2(lru_bi[0]), r2(lru_lambda[0]), b, tm, hd ** -0.5)

    wkv = w_kv[0]
    kt, vv = _kv(mem, wkv[:, :d].T.astype(BF16), wkv[:, d:].astype(BF16))

    yg = _gmlp(u, v, gmlp_spatial_w[0], gmlp_spatial_b[0].T, _tile(s, 512))

    yx = _xattn(q, kt, vv, b, _tile(s, 512))

    h1, lt = _merge(x2, lg, lb, yg, yl, yx, g0, g1, g2,
                    w_br_gmlp[0].astype(BF16), w_br_lru[0].astype(BF16), w_br_xa[0].astype(BF16),
                    w_out[0].astype(BF16), r2(ln1_g[0]), r2(ln1_b[0]), w_router[0].T, tm)

    tr = ROUTE_TILE
    assert t % tr == 0
    n_tiles = t // tr
    rs = tr * TOP_K + N_EXPERTS * RUN_ALIGN
    rb = jnp.broadcast_to(router_bias[0].astype(F32)[:, None], (N_EXPERTS, 128))
    pos, wk, run_len3, run_off3 = _route(lt, rb, tr)

    run_len = run_len3[:, :, 0]
    run_off = run_off3[:, :, 0]
    total = run_off[-1] + run_len[-1]
    seg_len = (total + EXPERT_BLOCK - 1) // EXPERT_BLOCK * EXPERT_BLOCK
    seg_end = jnp.cumsum(seg_len)
    seg_start = seg_end - seg_len
    run_row = seg_start[None, :] + run_off
    tile_rows = jnp.sum(run_len, axis=1).astype(jnp.int32)
    max_rows = t * TOP_K + n_tiles * N_EXPERTS * (RUN_ALIGN - 1)
    n_blocks = -(-max_rows // EXPERT_BLOCK) + N_EXPERTS
    block_start = jnp.arange(n_blocks, dtype=jnp.int32) * EXPERT_BLOCK
    block_e = jnp.minimum(jnp.sum(block_start[:, None] >= seg_end[None, :], axis=1),
                          N_EXPERTS - 1).astype(jnp.int32)
    n_used = (seg_end[-1:] // EXPERT_BLOCK).astype(jnp.int32)
    n_tail = n_blocks - (t * TOP_K) // EXPERT_BLOCK
    tail_blk = n_used[0] + jnp.arange(n_tail, dtype=jnp.int32)
    zero_blocks = jnp.concatenate([
        jnp.where(total % EXPERT_BLOCK != 0, seg_end // EXPERT_BLOCK - 1, -1),
        jnp.where(tail_blk < n_blocks, tail_blk, -1)]).astype(jnp.int32)

    run_len_flat = run_len.reshape(-1).astype(jnp.int32)
    run_row_flat = run_row.reshape(-1).astype(jnp.int32)
    xs = _dispatch(run_len_flat, run_row_flat, tile_rows, zero_blocks, pos, h1,
                   n_blocks * EXPERT_BLOCK, tr, rs)
    ys = _experts(block_e, n_used, xs, exp_gate[0], exp_up[0], exp_down[0])
    sgu = jnp.concatenate([sh_gate[0], sh_up[0]], axis=-1).astype(BF16)
    out = _combine(run_len_flat, run_row_flat, tile_rows, h1, pos, wk, ys, sgu, sh_down[0].astype(BF16),
                   r2(ln2_g[0]), r2(ln2_b[0]), tr, rs)
    return out.reshape(b, s, d)
```

```python
import functools

import jax
import jax.numpy as jnp
from jax import lax
from jax.experimental import pallas as pl
from jax.experimental.pallas import tpu as pltpu

F32 = jnp.float32
BF16 = jnp.bfloat16

GMLP_GROUPS = 8
GMLP_CHUNK = 128
LRU_BLOCKS = 8
CONV_WIDTH = 4
LRU_C = 8.0
XA_HEADS = 4
N_EXPERTS = 64
TOP_K = 8
N_GROUPS = 8
TOPK_GROUPS = 4
ROUTED_SCALE = 2.5
EXPERT_BLOCK = 1024
ROUTE_TILE = 256
RUN_ALIGN = 16
SORT_CHUNK = 256
RUN_ISSUE_UNROLL = 4
SCAN_GROUP = 8
LRU_ROWS = 128
LN_EPS = 1e-5
DEPTH = 1
DN_ALPHA = (2.0 * DEPTH) ** 0.25

VMEM_LIMIT_V7X = 56 * 1024 * 1024


def _cparams(sem):
    return pltpu.CompilerParams(dimension_semantics=sem, vmem_limit_bytes=VMEM_LIMIT_V7X)


def _ln(x, g, b):
    mu = jnp.mean(x, axis=-1, keepdims=True)
    xc = x - mu
    var = jnp.mean(xc * xc, axis=-1, keepdims=True)
    return xc * lax.rsqrt(var + LN_EPS) * g + b


def _gelu(x):
    return x * (0.5 * (1.0 + jnp.tanh(0.7978845608028654 * (x + 0.044715 * (x * x * x)))))


def _sigmoid(x):
    return 0.5 * (1.0 + jnp.tanh(0.5 * x))


def _dot(a, b):
    return jnp.dot(a, b, preferred_element_type=F32)


def _lru_block(blk, r0, xe_ref, gate_ref, cw_ref, cb_ref, wai_ref, ba_ref, bi_ref, lam_ref, hc_ref, y_ref,
               *, nr, bd):
    cs = slice(blk * bd, (blk + 1) * bd)
    xc = cb_ref[:, cs] + jnp.zeros((nr, bd), F32)
    for k in range(CONV_WIDTH):
        xc = xc + cw_ref[k:k + 1, cs] * xe_ref[8 + r0 - k:8 + r0 - k + nr, cs]
    ri = _dot(xc.astype(BF16), wai_ref[blk])
    r = _sigmoid(ri[:, :bd] + ba_ref[:, cs])
    ig = _sigmoid(ri[:, bd:] + bi_ref[:, cs])
    lam = lam_ref[:, cs]
    softplus_neg = jnp.maximum(-lam, 0.0) + jnp.log1p(jnp.exp(-jnp.abs(lam)))
    log_a = (-LRU_C) * r * softplus_neg
    a = jnp.exp(log_a)
    mult = jnp.sqrt(1.0 - a * a)
    b = mult * (ig * xc)
    row_in_group = lax.broadcasted_iota(jnp.int32, (nr, bd), 0) % SCAN_GROUP
    dsh = 1
    while dsh < SCAN_GROUP:
        a_sh = pltpu.roll(a, dsh, 0)
        b_sh = pltpu.roll(b, dsh, 0)
        keep = row_in_group >= dsh
        b = jnp.where(keep, a * b_sh + b, b)
        a = jnp.where(keep, a * a_sh, a)
        dsh *= 2
    carry = hc_ref[0:1, cs]
    hs = []
    for g in range(nr // SCAN_GROUP):
        rows = slice(g * SCAN_GROUP, (g + 1) * SCAN_GROUP)
        hs.append(b[rows, :] + a[rows, :] * carry)
        carry = hs[-1][SCAN_GROUP - 1:SCAN_GROUP, :]
    hc_ref[0:1, cs] = carry
    y_ref[r0:r0 + nr, cs] = (jnp.concatenate(hs, axis=0) * gate_ref[r0:r0 + nr, cs]).astype(BF16)


def _inproj_kernel(x_ref, lg_ref, lb_ref, w_ref, ng_ref, nb_ref, cw_ref, cb_ref, wai_ref, ba_ref, bi_ref, lam_ref,
                   wsp_ref, bspt_ref, kt_ref, vv_ref,
                   yg_ref, yl_ref, yx_ref, g0_ref, g1_ref, g2_ref,
                   xe_ref, hc_ref, gate_ref, u_ref, v_ref, q_ref, *, d, q_scale):
    @pl.when(pl.program_id(1) == 0)
    def _():
        xe_ref[0:8, :] = jnp.zeros((8, d), F32)
        hc_ref[...] = jnp.zeros(hc_ref.shape, F32)

    tm = x_ref.shape[0]
    hb = _ln(x_ref[...], lg_ref[...], lb_ref[...]).astype(BF16)

    def z(j):
        return _dot(hb, w_ref[:, j * d:(j + 1) * d])

    def lru(blk):
        for r0 in range(0, tm, LRU_ROWS):
            _lru_block(blk, r0, xe_ref, gate_ref, cw_ref, cb_ref, wai_ref, ba_ref, bi_ref, lam_ref, hc_ref,
                       yl_ref, nr=LRU_ROWS, bd=d // LRU_BLOCKS)

    xe_ref[8:8 + tm, :] = z(2)
    gate_ref[...] = _gelu(z(3))
    u_ref[...] = _gelu(z(0)).astype(BF16)
    lru(0), lru(1)
    v_ref[...] = _ln(_gelu(z(1)), ng_ref[...], nb_ref[...]).astype(BF16)
    lru(2), lru(3)
    q_ref[...] = (z(4) * q_scale).astype(BF16)
    lru(4)
    g0_ref[...] = _sigmoid(z(5)).astype(BF16)
    lru(5)
    g1_ref[...] = _sigmoid(z(6)).astype(BF16)
    lru(6)
    g2_ref[...] = _sigmoid(z(7)).astype(BF16)
    lru(7)
    xe_ref[0:8, :] = xe_ref[tm:tm + 8, :]
    _gmlp_tile(u_ref, v_ref, wsp_ref, bspt_ref, yg_ref, n_chunks=tm // GMLP_CHUNK)
    _xattn_tile(q_ref, kt_ref, vv_ref, yx_ref, hd=d // XA_HEADS)


def _inproj(x2, ln_g, ln_b, w_in, ng, nb, conv_w, conv_b, wai, ba, bi, lam, w_sp, b_sp_t, kt, vv, batch, tm,
            q_scale):
    t, d = x2.shape
    n_s = t // batch // tm
    n_out = 6
    row = pl.BlockSpec((tm, d), lambda b, s: (b * n_s + s, 0))
    vec = pl.BlockSpec((1, d), lambda b, s: (0, 0))
    wspec = pl.BlockSpec(w_in.shape, lambda b, s: (0, 0), pipeline_mode=pl.Buffered(1))
    return pl.pallas_call(
        functools.partial(_inproj_kernel, d=d, q_scale=q_scale),
        grid=(batch, n_s),
        in_specs=[row, vec, vec, wspec, vec, vec,
                  pl.BlockSpec(conv_w.shape, lambda b, s: (0, 0)), vec,
                  pl.BlockSpec(wai.shape, lambda b, s: (0, 0, 0)), vec, vec, vec,
                  pl.BlockSpec(w_sp.shape, lambda b, s: (0, 0, 0)),
                  pl.BlockSpec(b_sp_t.shape, lambda b, s: (0, 0)),
                  pl.BlockSpec((1,) + kt.shape[1:], lambda b, s: (b, 0, 0)),
                  pl.BlockSpec((1,) + vv.shape[1:], lambda b, s: (b, 0, 0))],
        out_specs=[row] * n_out,
        out_shape=[jax.ShapeDtypeStruct((t, d), BF16)] * n_out,
        scratch_shapes=[pltpu.VMEM((tm + 8, d), F32), pltpu.VMEM((8, d), F32), pltpu.VMEM((tm, d), F32),
                        pltpu.VMEM((tm, d), BF16), pltpu.VMEM((tm, d), BF16), pltpu.VMEM((tm, d), BF16)],
        compiler_params=_cparams(("arbitrary", "arbitrary")),
        name="inproj",
    )(x2, ln_g, ln_b, w_in, ng, nb, conv_w, conv_b, wai, ba, bi, lam, w_sp, b_sp_t, kt, vv)


def _kv_kernel(mem_ref, wkt_ref, wv_ref, kt_ref, v_ref):
    m = mem_ref[0].astype(BF16)
    kt = lax.dot_general(wkt_ref[...], m, (((1,), (1,)), ((), ())), preferred_element_type=F32)
    kt_ref[0] = kt.astype(BF16)
    v_ref[0] = _dot(m, wv_ref[...]).astype(BF16)


def _kv(mem, wkt, wv):
    b, m, d = mem.shape
    xw = wv.shape[1]
    return pl.pallas_call(
        _kv_kernel,
        grid=(b,),
        in_specs=[pl.BlockSpec((1, m, d), lambda i: (i, 0, 0)),
                  pl.BlockSpec(wkt.shape, lambda i: (0, 0)),
                  pl.BlockSpec(wv.shape, lambda i: (0, 0))],
        out_specs=[pl.BlockSpec((1, xw, m), lambda i: (i, 0, 0)),
                   pl.BlockSpec((1, m, xw), lambda i: (i, 0, 0))],
        out_shape=[jax.ShapeDtypeStruct((b, xw, m), BF16), jax.ShapeDtypeStruct((b, m, xw), BF16)],
        compiler_params=_cparams(("parallel",)),
        name="kv",
    )(mem, wkt, wv)


def _gmlp_tile(u_ref, v_ref, w_ref, bt_ref, y_ref, *, n_chunks):
    c = GMLP_CHUNK
    rows = lax.broadcasted_iota(jnp.int32, (c, c), 0)
    cols = lax.broadcasted_iota(jnp.int32, (c, c), 1)
    causal = cols <= rows
    for g in range(GMLP_GROUPS):
        wg = jnp.where(causal, w_ref[g], 0.0).astype(BF16)
        bcol = bt_ref[:, g:g + 1]
        for n in range(n_chunks):
            vb = v_ref[n * c:(n + 1) * c, g * c:(g + 1) * c]
            mixed = _dot(wg, vb) + bcol
            ub = u_ref[n * c:(n + 1) * c, g * c:(g + 1) * c].astype(F32)
            y_ref[n * c:(n + 1) * c, g * c:(g + 1) * c] = (ub * mixed).astype(BF16)


def _xattn_tile(q_ref, kt_ref, v_ref, y_ref, *, hd):
    for h in range(XA_HEADS):
        cs = slice(h * hd, (h + 1) * hd)
        sc = _dot(q_ref[:, cs], kt_ref[0, cs, :])
        m = jnp.max(sc, axis=-1, keepdims=True)
        e = jnp.exp(sc - m)
        l = jnp.sum(e, axis=-1, keepdims=True)
        o = _dot(e.astype(BF16), v_ref[0, :, cs])
        y_ref[:, cs] = (o / l).astype(BF16)


def _first_argmax(x, rowf, n):
    m = jnp.max(x, axis=0, keepdims=True)
    first = jnp.min(jnp.where(x == m, rowf, float(n)), axis=0, keepdims=True)
    return m, first


def _strict_lower(n, m, transpose=False):
    r = lax.broadcasted_iota(jnp.int32, (n, m), 0)
    c = lax.broadcasted_iota(jnp.int32, (n, m), 1)
    return jnp.where((r < c) if transpose else (c < r), 1.0, 0.0).astype(BF16)


def _route_tile(lt, rb_ref, run_ref):
    tr = lt.shape[1]
    gsz = N_EXPERTS // N_GROUPS
    s = _sigmoid(lt)
    choice = s + rb_ref[:, 0:1]
    neg_inf = -jnp.inf

    rowg = lax.broadcasted_iota(jnp.int32, (gsz, tr), 0).astype(F32)
    gs_rows = []
    for g in range(N_GROUPS):
        cg = choice[g * gsz:(g + 1) * gsz, :]
        m1, f1 = _first_argmax(cg, rowg, gsz)
        m2 = jnp.max(jnp.where(rowg == f1, neg_inf, cg), axis=0, keepdims=True)
        gs_rows.append(m1 + m2)
    gscore = jnp.concatenate(gs_rows, axis=0)

    rowG = lax.broadcasted_iota(jnp.int32, (N_GROUPS, tr), 0).astype(F32)
    gmask = jnp.zeros((N_GROUPS, tr), F32)
    for _ in range(TOPK_GROUPS):
        _, f = _first_argmax(gscore, rowG, N_GROUPS)
        hit = rowG == f
        gmask = jnp.where(hit, 1.0, gmask)
        gscore = jnp.where(hit, neg_inf, gscore)
    emask = jnp.concatenate(
        [jnp.broadcast_to(gmask[g:g + 1, :], (gsz, tr)) for g in range(N_GROUPS)], axis=0)

    rowE = lax.broadcasted_iota(jnp.int32, (N_EXPERTS, tr), 0).astype(F32)
    masked = jnp.where(emask > 0.5, choice, neg_inf)
    hits = []
    self32 = jnp.zeros((N_EXPERTS, tr), F32)
    for _ in range(TOP_K):
        _, f = _first_argmax(masked, rowE, N_EXPERTS)
        hit = rowE == f
        hits.append(hit)
        self32 = jnp.where(hit, 1.0, self32)
        masked = jnp.where(hit, neg_inf, masked)

    wsel = self32 * s
    denom = jnp.sum(wsel, axis=0, keepdims=True)
    w = wsel / denom * ROUTED_SCALE

    rank = _dot(self32.astype(BF16), _strict_lower(tr, tr, transpose=True))
    n_e = jnp.sum(self32, axis=1, keepdims=True)
    run_len = jnp.floor((n_e + (RUN_ALIGN - 1)) * (1.0 / RUN_ALIGN)) * RUN_ALIGN
    run_len_b = jnp.broadcast_to(run_len, (N_EXPERTS, 128))
    run_start = _dot(_strict_lower(N_EXPERTS, N_EXPERTS), run_len_b.astype(BF16))[:, 0:1]
    pos = rank + run_start

    run_off = run_ref[...].astype(jnp.int32)
    run_ref[...] = run_ref[...] + run_len_b
    wk = jnp.concatenate([jnp.sum(jnp.where(hit, w, 0.0), axis=0, keepdims=True) for hit in hits], axis=0)
    pk = jnp.concatenate([jnp.sum(jnp.where(hit, pos, 0.0), axis=0, keepdims=True) for hit in hits], axis=0)
    return pk.astype(jnp.int32), wk, run_len_b.astype(jnp.int32), run_off


def _merge_kernel(x_ref, lg_ref, lb_ref, yg_ref, yl_ref, yx_ref, g0_ref, g1_ref, g2_ref,
                  w0_ref, w1_ref, w2_ref, wo_ref, l1g_ref, l1b_ref, wrt_ref, rb_ref,
                  h1_ref, pos_ref, wk_ref, len_ref, off_ref, run_ref):
    @pl.when(pl.program_id(0) == 0)
    def _():
        run_ref[...] = jnp.zeros(run_ref.shape, F32)

    h0 = _ln(x_ref[...], lg_ref[...], lb_ref[...])
    merged = (g0_ref[...].astype(F32) * _dot(yg_ref[...], w0_ref[...])
              + g1_ref[...].astype(F32) * _dot(yl_ref[...], w1_ref[...])
              + g2_ref[...].astype(F32) * _dot(yx_ref[...], w2_ref[...]))
    h1 = _ln(DN_ALPHA * h0 + _dot(merged.astype(BF16), wo_ref[...]), l1g_ref[...], l1b_ref[...])
    h1_ref[...] = h1
    lt = lax.dot_general(wrt_ref[...], h1, (((1,), (1,)), ((), ())),
                         precision=lax.Precision.HIGHEST, preferred_element_type=F32)
    for sub in range(h1.shape[0] // ROUTE_TILE):
        cols = slice(sub * ROUTE_TILE, (sub + 1) * ROUTE_TILE)
        pos, wk, run_len, run_off = _route_tile(lt[:, cols], rb_ref, run_ref)
        pos_ref[:, cols] = pos
        wk_ref[:, cols] = wk
        len_ref[sub] = run_len
        off_ref[sub] = run_off


def _merge(x2, ln_g, ln_b, yg, yl, yx, g0, g1, g2, w0, w1, w2, wo, l1g, l1b, wrt, rb, tm):
    t, d = x2.shape
    e = wrt.shape[0]
    sub = tm // ROUTE_TILE
    row = pl.BlockSpec((tm, d), lambda i: (i, 0))
    vec = pl.BlockSpec((1, d), lambda i: (0, 0))
    wsq = pl.BlockSpec((d, d), lambda i: (0, 0))
    kblk = pl.BlockSpec((TOP_K, tm), lambda i: (0, i))
    tblk = pl.BlockSpec((sub, e, 128), lambda i: (i, 0, 0))
    return pl.pallas_call(
        _merge_kernel,
        grid=(t // tm,),
        in_specs=[row, vec, vec, row, row, row, row, row, row, wsq, wsq, wsq, wsq, vec, vec,
                  pl.BlockSpec(wrt.shape, lambda i: (0, 0)), pl.BlockSpec(rb.shape, lambda i: (0, 0))],
        out_specs=[row, kblk, kblk, tblk, tblk],
        out_shape=[jax.ShapeDtypeStruct((t, d), F32),
                   jax.ShapeDtypeStruct((TOP_K, t), jnp.int32),
                   jax.ShapeDtypeStruct((TOP_K, t), F32),
                   jax.ShapeDtypeStruct((t // ROUTE_TILE, e, 128), jnp.int32),
                   jax.ShapeDtypeStruct((t // ROUTE_TILE, e, 128), jnp.int32)],
        scratch_shapes=[pltpu.VMEM((e, 128), F32)],
        compiler_params=_cparams(("arbitrary",)),
        name="merge",
    )(x2, ln_g, ln_b, yg, yl, yx, g0, g1, g2, w0, w1, w2, wo, l1g, l1b, wrt, rb)


def _run_copy(vmem_buf, hbm_ref, off, row, n, sem, to_hbm):
    n = pl.multiple_of(n, RUN_ALIGN)
    v = vmem_buf.at[pl.ds(pl.multiple_of(off, RUN_ALIGN), n), :]
    h = hbm_ref.at[pl.ds(pl.multiple_of(row, RUN_ALIGN), n), :]
    return pltpu.make_async_copy(v, h, sem) if to_hbm else pltpu.make_async_copy(h, v, sem)


def _start_runs(len_ref, row_ref, tile, vmem_buf, hbm_ref, sem, *, to_hbm):
    def body(i, off):
        for u in range(RUN_ISSUE_UNROLL):
            e = i * RUN_ISSUE_UNROLL + u
            n = len_ref[tile * N_EXPERTS + e]

            @pl.when(n > 0)
            def _():
                _run_copy(vmem_buf, hbm_ref, off, row_ref[tile * N_EXPERTS + e], n, sem, to_hbm).start()
            off = off + n
        return off

    lax.fori_loop(0, N_EXPERTS // RUN_ISSUE_UNROLL, body, jnp.int32(0))


def _wait_runs(rows_ref, tile, vmem_buf, hbm_ref, sem, *, to_hbm):
    n = rows_ref[tile]

    @pl.when(n > 0)
    def _():
        _run_copy(vmem_buf, hbm_ref, 0, 0, n, sem, to_hbm).wait()


def _chunks_always(tr):
    return (tr * TOP_K + N_EXPERTS * RUN_ALIGN // 2) // SORT_CHUNK


def _assignment_rows(pos_ref, val_ref, chunk, n_rows):
    tr = pos_ref.shape[1]
    rows = lax.broadcasted_iota(jnp.int32, (n_rows, tr), 0).astype(F32).astype(BF16)
    m = jnp.zeros((n_rows, tr), BF16)
    for k in range(TOP_K):
        rel = (pos_ref[k:k + 1, :] - chunk * n_rows).astype(F32).astype(BF16)
        v = jnp.ones((1, tr), BF16) if val_ref is None else val_ref[k:k + 1, :].astype(BF16)
        m = jnp.where(rows == rel, v, m)
    return m


def _dispatch_kernel(len_ref, row_ref, rows_ref, zb_ref, pos_ref, h1_ref, xs_ref, buf, zbuf, h1b, sem, zsem, *, rs):
    tile = pl.program_id(0)
    n_tiles = pl.num_programs(0)
    slot = tile % 2

    def zero_fill(group, lo, hi, *, wait):
        def zcopy(j):
            return pltpu.make_async_copy(
                zbuf, xs_ref.at[pl.ds(zb_ref[j] * EXPERT_BLOCK, EXPERT_BLOCK), :], zsem.at[group])

        def body(j, carry):
            @pl.when(zb_ref[j] >= 0)
            def _():
                zcopy(j).wait() if wait else zcopy(j).start()
            return carry

        lax.fori_loop(lo, hi, body, 0)

    @pl.when(tile == 0)
    def _():
        zbuf[...] = jnp.zeros(zbuf.shape, zbuf.dtype)
        zero_fill(0, 0, N_EXPERTS, wait=False)
        zero_fill(1, N_EXPERTS, zb_ref.shape[0], wait=False)
        zero_fill(0, 0, N_EXPERTS, wait=True)

    @pl.when(tile >= 2)
    def _():
        _wait_runs(rows_ref, tile - 2, buf.at[slot], xs_ref, sem.at[slot], to_hbm=True)

    h1b[...] = h1_ref[...].astype(BF16)

    def sort_chunk(c):
        onehot = _assignment_rows(pos_ref, None, c, SORT_CHUNK)
        srt = _dot(onehot, h1b[...])
        buf[slot, c * SORT_CHUNK:(c + 1) * SORT_CHUNK, :] = srt.astype(BF16)

    n_always = _chunks_always(h1_ref.shape[0])
    for c in range(n_always):
        sort_chunk(c)
    for c in range(n_always, rs // SORT_CHUNK):
        pl.when(rows_ref[tile] > c * SORT_CHUNK)(functools.partial(sort_chunk, c))
    _start_runs(len_ref, row_ref, tile, buf.at[slot], xs_ref, sem.at[slot], to_hbm=True)

    @pl.when(tile == n_tiles - 1)
    def _():
        @pl.when(tile >= 1)
        def _():
            _wait_runs(rows_ref, tile - 1, buf.at[1 - slot], xs_ref, sem.at[1 - slot], to_hbm=True)
        _wait_runs(rows_ref, tile, buf.at[slot], xs_ref, sem.at[slot], to_hbm=True)
        zero_fill(1, N_EXPERTS, zb_ref.shape[0], wait=True)


def _dispatch(run_len, run_row, tile_rows, zero_blocks, pos, h1, n_rows, tr, rs):
    t, d = h1.shape
    return pl.pallas_call(
        functools.partial(_dispatch_kernel, rs=rs),
        grid_spec=pltpu.PrefetchScalarGridSpec(
            num_scalar_prefetch=4, grid=(t // tr,),
            in_specs=[pl.BlockSpec((TOP_K, tr), lambda i, *_: (0, i)),
                      pl.BlockSpec((tr, d), lambda i, *_: (i, 0))],
            out_specs=pl.BlockSpec(memory_space=pl.ANY),
            scratch_shapes=[pltpu.VMEM((2, rs, d), BF16),
                            pltpu.VMEM((EXPERT_BLOCK, d), BF16),
                            pltpu.VMEM((tr, d), BF16),
                            pltpu.SemaphoreType.DMA((2,)), pltpu.SemaphoreType.DMA((2,))]),
        out_shape=jax.ShapeDtypeStruct((n_rows, d), BF16),
        compiler_params=_cparams(("arbitrary",)),
        name="dispatch",
    )(run_len, run_row, tile_rows, zero_blocks, pos, h1)


def _expert_kernel(be_ref, nu_ref, xs_ref, wg_ref, wu_ref, wd_ref, ys_ref, wgu_s, wd_s, *, ed):
    i = pl.program_id(0)
    live = i < nu_ref[0]
    new_expert = jnp.logical_or(i == 0, be_ref[i] != be_ref[jnp.maximum(i - 1, 0)])

    @pl.when(jnp.logical_and(live, new_expert))
    def _():
        wgu_s[:, :ed] = wg_ref[0].astype(BF16)
        wgu_s[:, ed:] = wu_ref[0].astype(BF16)
        wd_s[...] = wd_ref[0].astype(BF16)

    @pl.when(live)
    def _():
        gu = _dot(xs_ref[...], wgu_s[...])
        g = gu[:, :ed]
        act = (g * _sigmoid(g) * gu[:, ed:]).astype(BF16)
        ys_ref[...] = _dot(act, wd_s[...]).astype(BF16)

    @pl.when(jnp.logical_not(live))
    def _():
        ys_ref[...] = jnp.zeros(ys_ref.shape, ys_ref.dtype)


def _experts(block_e, n_used, xs, wg, wu, wd):
    n_rows, d = xs.shape
    n_blocks = n_rows // EXPERT_BLOCK
    ed = wg.shape[2]

    def xmap(i, be, nu):
        return (jnp.minimum(i, nu[0] - 1), 0)

    def ymap(i, be, nu):
        return (i, 0)

    def wmap(i, be, nu):
        return (be[i], 0, 0)

    return pl.pallas_call(
        functools.partial(_expert_kernel, ed=ed),
        grid_spec=pltpu.PrefetchScalarGridSpec(
            num_scalar_prefetch=2, grid=(n_blocks,),
            in_specs=[pl.BlockSpec((EXPERT_BLOCK, d), xmap),
                      pl.BlockSpec((1, d, ed), wmap),
                      pl.BlockSpec((1, d, ed), wmap),
                      pl.BlockSpec((1, ed, d), wmap)],
            out_specs=pl.BlockSpec((EXPERT_BLOCK, d), ymap),
            scratch_shapes=[pltpu.VMEM((d, 2 * ed), BF16), pltpu.VMEM((ed, d), BF16)]),
        out_shape=jax.ShapeDtypeStruct((n_rows, d), BF16),
        compiler_params=_cparams(("arbitrary",)),
        name="experts",
    )(block_e, n_used, xs, wg, wu, wd)


def _combine_kernel(len_ref, row_ref, rows_ref, h1_ref, pos_ref, wk_ref, ys_ref, sgu_ref, sd_ref,
                    l2g_ref, l2b_ref, out_ref, ybuf, wmt, sem, *, rs, sdim):
    tile = pl.program_id(0)
    n_tiles = pl.num_programs(0)
    slot = tile % 2

    @pl.when(tile == 0)
    def _():
        ybuf[...] = jnp.zeros(ybuf.shape, ybuf.dtype)
        _start_runs(len_ref, row_ref, tile, ybuf.at[0], ys_ref, sem.at[0], to_hbm=False)

    @pl.when(tile + 1 < n_tiles)
    def _():
        _start_runs(len_ref, row_ref, tile + 1, ybuf.at[1 - slot], ys_ref, sem.at[1 - slot], to_hbm=False)

    _wait_runs(rows_ref, tile, ybuf.at[slot], ys_ref, sem.at[slot], to_hbm=False)

    h1 = h1_ref[...]
    gu = _dot(h1.astype(BF16), sgu_ref[...])
    g = gu[:, :sdim]
    act = (g * _sigmoid(g) * gu[:, sdim:]).astype(BF16)
    acc = DN_ALPHA * h1 + _dot(act, sd_ref[...])

    def weights_chunk(c):
        wmt[c * SORT_CHUNK:(c + 1) * SORT_CHUNK, :] = _assignment_rows(pos_ref, wk_ref, c, SORT_CHUNK)

    def unsort(lo, hi):
        return lax.dot_general(wmt[lo:hi, :], ybuf[slot, lo:hi, :], (((0,), (0,)), ((), ())),
                               preferred_element_type=F32)

    n_always = _chunks_always(h1.shape[0])
    for c in range(n_always):
        weights_chunk(c)
    out_ref[...] = acc + unsort(0, n_always * SORT_CHUNK)
    for c in range(n_always, rs // SORT_CHUNK):
        @pl.when(rows_ref[tile] > c * SORT_CHUNK)
        def _(c=c):
            weights_chunk(c)
            out_ref[...] += unsort(c * SORT_CHUNK, (c + 1) * SORT_CHUNK)
    out_ref[...] = _ln(out_ref[...], l2g_ref[...], l2b_ref[...])


def _combine(run_len, run_row, tile_rows, h1, pos, wk, ys, sgu, sd, l2g, l2b, tr, rs):
    t, d = h1.shape
    sdim = sd.shape[0]
    row = pl.BlockSpec((tr, d), lambda i, *_: (i, 0))
    krow = pl.BlockSpec((TOP_K, tr), lambda i, *_: (0, i))
    vec = pl.BlockSpec((1, d), lambda i, *_: (0, 0))
    return pl.pallas_call(
        functools.partial(_combine_kernel, rs=rs, sdim=sdim),
        grid_spec=pltpu.PrefetchScalarGridSpec(
            num_scalar_prefetch=3, grid=(t // tr,),
            in_specs=[row, krow, krow,
                      pl.BlockSpec(memory_space=pl.ANY),
                      pl.BlockSpec(sgu.shape, lambda i, *_: (0, 0)),
                      pl.BlockSpec(sd.shape, lambda i, *_: (0, 0)),
                      vec, vec],
            out_specs=row,
            scratch_shapes=[pltpu.VMEM((2, rs, d), BF16),
                            pltpu.VMEM((rs, tr), BF16),
                            pltpu.SemaphoreType.DMA((2,))]),
        out_shape=jax.ShapeDtypeStruct((t, d), F32),
        compiler_params=_cparams(("arbitrary",)),
        name="combine",
    )(run_len, run_row, tile_rows, h1, pos, wk, ys, sgu, sd, l2g, l2b)


def _tile(n, pref):
    while n % pref:
        pref //= 2
    return pref


def kernel(x, mem, ln_in_g, ln_in_b, w_in, gmlp_norm_g, gmlp_norm_b, gmlp_spatial_w, gmlp_spatial_b, conv_w, conv_b, lru_wa, lru_ba, lru_wi, lru_bi, lru_lambda, w_kv, w_br_gmlp, w_br_lru, w_br_xa, w_out, ln1_g, ln1_b, w_router, router_bias, exp_gate, exp_up, exp_down, sh_gate, sh_up, sh_down, ln2_g, ln2_b):
    b, s, d = x.shape
    t = b * s
    assert w_in.shape[0] == DEPTH and w_in.shape[2] == 8 * d and s % GMLP_CHUNK == 0
    r2 = lambda a: a.reshape(1, -1)
    x2 = x.reshape(t, d)
    lg, lb = r2(ln_in_g), r2(ln_in_b)
    hd = d // XA_HEADS

    wkv = w_kv[0]
    kt, vv = _kv(mem, wkv[:, :d].T.astype(BF16), wkv[:, d:].astype(BF16))

    tm = _tile(s, 512)
    assert tm % GMLP_CHUNK == 0
    wai = jnp.concatenate([lru_wa[0], lru_wi[0]], axis=-1).astype(BF16)
    yg, yl, yx, g0, g1, g2 = _inproj(
        x2, lg, lb, w_in[0].astype(BF16), r2(gmlp_norm_g[0]), r2(gmlp_norm_b[0]),
        conv_w[0], r2(conv_b[0]), wai, r2(lru_ba[0]), r2(lru_bi[0]), r2(lru_lambda[0]),
        gmlp_spatial_w[0], gmlp_spatial_b[0].T, kt, vv, b, tm, hd ** -0.5)

    tr = ROUTE_TILE
    assert tm % tr == 0
    n_tiles = t // tr
    rs = tr * TOP_K + N_EXPERTS * RUN_ALIGN
    rb = jnp.broadcast_to(router_bias[0].astype(F32)[:, None], (N_EXPERTS, 128))
    h1, pos, wk, run_len3, run_off3 = _merge(
        x2, lg, lb, yg, yl, yx, g0, g1, g2,
        w_br_gmlp[0].astype(BF16), w_br_lru[0].astype(BF16), w_br_xa[0].astype(BF16),
        w_out[0].astype(BF16), r2(ln1_g[0]), r2(ln1_b[0]), w_router[0].T, rb, tm)

    run_len = run_len3[:, :, 0]
    run_off = run_off3[:, :, 0]
    total = run_off[-1] + run_len[-1]
    seg_len = (total + EXPERT_BLOCK - 1) // EXPERT_BLOCK * EXPERT_BLOCK
    seg_end = jnp.cumsum(seg_len)
    seg_start = seg_end - seg_len
    run_row = seg_start[None, :] + run_off
    tile_rows = jnp.sum(run_len, axis=1).astype(jnp.int32)
    max_rows = t * TOP_K + n_tiles * N_EXPERTS * (RUN_ALIGN - 1)
    n_blocks = -(-max_rows // EXPERT_BLOCK) + N_EXPERTS
    block_start = jnp.arange(n_blocks, dtype=jnp.int32) * EXPERT_BLOCK
    block_e = jnp.minimum(jnp.sum(block_start[:, None] >= seg_end[None, :], axis=1),
                          N_EXPERTS - 1).astype(jnp.int32)
    n_used = (seg_end[-1:] // EXPERT_BLOCK).astype(jnp.int32)
    n_tail = n_blocks - (t * TOP_K) // EXPERT_BLOCK
    tail_blk = n_used[0] + jnp.arange(n_tail, dtype=jnp.int32)
    zero_blocks = jnp.concatenate([
        jnp.where(total % EXPERT_BLOCK != 0, seg_end // EXPERT_BLOCK - 1, -1),
        jnp.where(tail_blk < n_blocks, tail_blk, -1)]).astype(jnp.int32)

    run_len_flat = run_len.reshape(-1).astype(jnp.int32)
    run_row_flat = run_row.reshape(-1).astype(jnp.int32)
    xs = _dispatch(run_len_flat, run_row_flat, tile_rows, zero_blocks, pos, h1,
                   n_blocks * EXPERT_BLOCK, tr, rs)
    ys = _experts(block_e, n_used, xs, exp_gate[0], exp_up[0], exp_down[0])
    sgu = jnp.concatenate([sh_gate[0], sh_up[0]], axis=-1).astype(BF16)
    out = _combine(run_len_flat, run_row_flat, tile_rows, h1, pos, wk, ys, sgu, sh_down[0].astype(BF16),
                   r2(ln2_g[0]), r2(ln2_b[0]), tr, rs)
    return out.reshape(b, s, d)
```

```python
import functools

import jax
import jax.numpy as jnp
from jax import lax
from jax.experimental import pallas as pl
from jax.experimental.pallas import tpu as pltpu

F32 = jnp.float32
BF16 = jnp.bfloat16

GMLP_GROUPS = 8
GMLP_CHUNK = 128
LRU_BLOCKS = 8
CONV_WIDTH = 4
LRU_C = 8.0
XA_HEADS = 4
N_EXPERTS = 64
TOP_K = 8
N_GROUPS = 8
TOPK_GROUPS = 4
ROUTED_SCALE = 2.5
EXPERT_BLOCK = 1024
ROUTE_TILE = 256
RUN_ALIGN = 16
SORT_CHUNK = 256
RUN_ISSUE_UNROLL = 4
SCAN_GROUP = 8
LRU_ROWS = 128
LN_EPS = 1e-5
DEPTH = 1
DN_ALPHA = (2.0 * DEPTH) ** 0.25

VMEM_LIMIT_V7X = 56 * 1024 * 1024


def _cparams(sem):
    return pltpu.CompilerParams(dimension_semantics=sem, vmem_limit_bytes=VMEM_LIMIT_V7X)


def _ln(x, g, b):
    mu = jnp.mean(x, axis=-1, keepdims=True)
    xc = x - mu
    var = jnp.mean(xc * xc, axis=-1, keepdims=True)
    return xc * lax.rsqrt(var + LN_EPS) * g + b


def _gelu(x):
    return x * (0.5 * (1.0 + jnp.tanh(0.7978845608028654 * (x + 0.044715 * (x * x * x)))))


def _sigmoid(x):
    return 0.5 * (1.0 + jnp.tanh(0.5 * x))


def _dot(a, b):
    return jnp.dot(a, b, preferred_element_type=F32)


def _lru_block(blk, r0, xe_ref, gate_ref, cw_ref, cb_ref, wai_ref, ba_ref, bi_ref, lam_ref, hc_ref, y_ref,
               *, nr, bd):
    cs = slice(blk * bd, (blk + 1) * bd)
    xc = cb_ref[:, cs] + jnp.zeros((nr, bd), F32)
    for k in range(CONV_WIDTH):
        xc = xc + cw_ref[k:k + 1, cs] * xe_ref[8 + r0 - k:8 + r0 - k + nr, cs]
    ri = _dot(xc.astype(BF16), wai_ref[blk])
    r = _sigmoid(ri[:, :bd] + ba_ref[:, cs])
    ig = _sigmoid(ri[:, bd:] + bi_ref[:, cs])
    lam = lam_ref[:, cs]
    softplus_neg = jnp.maximum(-lam, 0.0) + jnp.log1p(jnp.exp(-jnp.abs(lam)))
    log_a = (-LRU_C) * r * softplus_neg
    a = jnp.exp(log_a)
    om = 1.0 - a * a
    mult = jnp.where(om > 0.0, om * lax.rsqrt(om), 0.0)
    b = mult * (ig * xc)
    ng = nr // SCAN_GROUP
    a = a.reshape(ng, SCAN_GROUP, bd)
    b = b.reshape(ng, SCAN_GROUP, bd)
    row_in_group = lax.broadcasted_iota(jnp.int32, (ng, SCAN_GROUP, bd), 1)
    dsh = 1
    while dsh < SCAN_GROUP:
        a_sh = pltpu.roll(a, dsh, 1)
        b_sh = pltpu.roll(b, dsh, 1)
        keep = row_in_group >= dsh
        b = jnp.where(keep, a * b_sh + b, b)
        a = jnp.where(keep, a * a_sh, a)
        dsh *= 2
    carry = hc_ref[0:1, cs]
    hs = []
    for g in range(ng):
        hs.append(b[g] + a[g] * carry)
        carry = hs[-1][SCAN_GROUP - 1:SCAN_GROUP, :]
    hc_ref[0:1, cs] = carry
    y_ref[r0:r0 + nr, cs] = (jnp.concatenate(hs, axis=0) * gate_ref[r0:r0 + nr, cs]).astype(BF16)


def _inproj_kernel(x_ref, lg_ref, lb_ref, w_ref, ng_ref, nb_ref, cw_ref, cb_ref, wai_ref, ba_ref, bi_ref, lam_ref,
                   wsp_ref, bspt_ref, kt_ref, vv_ref,
                   yg_ref, yl_ref, yx_ref, g0_ref, g1_ref, g2_ref,
                   xe_ref, hc_ref, gate_ref, u_ref, v_ref, q_ref, *, d, q_scale):
    @pl.when(pl.program_id(1) == 0)
    def _():
        xe_ref[0:8, :] = jnp.zeros((8, d), F32)
        hc_ref[...] = jnp.zeros(hc_ref.shape, F32)

    tm = x_ref.shape[0]
    hb = _ln(x_ref[...], lg_ref[...], lb_ref[...]).astype(BF16)

    def z(j):
        return _dot(hb, w_ref[:, j * d:(j + 1) * d])

    def lru(blk):
        for r0 in range(0, tm, LRU_ROWS):
            _lru_block(blk, r0, xe_ref, gate_ref, cw_ref, cb_ref, wai_ref, ba_ref, bi_ref, lam_ref, hc_ref,
                       yl_ref, nr=LRU_ROWS, bd=d // LRU_BLOCKS)

    xe_ref[8:8 + tm, :] = z(2)
    gate_ref[...] = _gelu(z(3))
    u_ref[...] = _gelu(z(0)).astype(BF16)
    lru(0), lru(1)
    v_ref[...] = _ln(_gelu(z(1)), ng_ref[...], nb_ref[...]).astype(BF16)
    lru(2), lru(3)
    q_ref[...] = (z(4) * q_scale).astype(BF16)
    lru(4)
    g0_ref[...] = _sigmoid(z(5)).astype(BF16)
    lru(5)
    g1_ref[...] = _sigmoid(z(6)).astype(BF16)
    lru(6)
    g2_ref[...] = _sigmoid(z(7)).astype(BF16)
    lru(7)
    xe_ref[0:8, :] = xe_ref[tm:tm + 8, :]
    _gmlp_tile(u_ref, v_ref, wsp_ref, bspt_ref, yg_ref, n_chunks=tm // GMLP_CHUNK)
    _xattn_tile(q_ref, kt_ref, vv_ref, yx_ref, hd=d // XA_HEADS)


def _inproj(x2, ln_g, ln_b, w_in, ng, nb, conv_w, conv_b, wai, ba, bi, lam, w_sp, b_sp_t, kt, vv, batch, tm,
            q_scale):
    t, d = x2.shape
    n_s = t // batch // tm
    n_out = 6
    row = pl.BlockSpec((tm, d), lambda b, s: (b * n_s + s, 0))
    vec = pl.BlockSpec((1, d), lambda b, s: (0, 0))
    wspec = pl.BlockSpec(w_in.shape, lambda b, s: (0, 0), pipeline_mode=pl.Buffered(1))
    return pl.pallas_call(
        functools.partial(_inproj_kernel, d=d, q_scale=q_scale),
        grid=(batch, n_s),
        in_specs=[row, vec, vec, wspec, vec, vec,
                  pl.BlockSpec(conv_w.shape, lambda b, s: (0, 0)), vec,
                  pl.BlockSpec(wai.shape, lambda b, s: (0, 0, 0)), vec, vec, vec,
                  pl.BlockSpec(w_sp.shape, lambda b, s: (0, 0, 0)),
                  pl.BlockSpec(b_sp_t.shape, lambda b, s: (0, 0)),
                  pl.BlockSpec((1,) + kt.shape[1:], lambda b, s: (b, 0, 0)),
                  pl.BlockSpec((1,) + vv.shape[1:], lambda b, s: (b, 0, 0))],
        out_specs=[row] * n_out,
        out_shape=[jax.ShapeDtypeStruct((t, d), BF16)] * n_out,
        scratch_shapes=[pltpu.VMEM((tm + 8, d), F32), pltpu.VMEM((8, d), F32), pltpu.VMEM((tm, d), F32),
                        pltpu.VMEM((tm, d), BF16), pltpu.VMEM((tm, d), BF16), pltpu.VMEM((tm, d), BF16)],
        compiler_params=_cparams(("arbitrary", "arbitrary")),
        name="inproj",
    )(x2, ln_g, ln_b, w_in, ng, nb, conv_w, conv_b, wai, ba, bi, lam, w_sp, b_sp_t, kt, vv)


def _kv_kernel(mem_ref, wkt_ref, wv_ref, kt_ref, v_ref):
    m = mem_ref[0].astype(BF16)
    kt = lax.dot_general(wkt_ref[...], m, (((1,), (1,)), ((), ())), preferred_element_type=F32)
    kt_ref[0] = kt.astype(BF16)
    v_ref[0] = _dot(m, wv_ref[...]).astype(BF16)


def _kv(mem, wkt, wv):
    b, m, d = mem.shape
    xw = wv.shape[1]
    return pl.pallas_call(
        _kv_kernel,
        grid=(b,),
        in_specs=[pl.BlockSpec((1, m, d), lambda i: (i, 0, 0)),
                  pl.BlockSpec(wkt.shape, lambda i: (0, 0)),
                  pl.BlockSpec(wv.shape, lambda i: (0, 0))],
        out_specs=[pl.BlockSpec((1, xw, m), lambda i: (i, 0, 0)),
                   pl.BlockSpec((1, m, xw), lambda i: (i, 0, 0))],
        out_shape=[jax.ShapeDtypeStruct((b, xw, m), BF16), jax.ShapeDtypeStruct((b, m, xw), BF16)],
        compiler_params=_cparams(("parallel",)),
        name="kv",
    )(mem, wkt, wv)


def _gmlp_tile(u_ref, v_ref, w_ref, bt_ref, y_ref, *, n_chunks):
    c = GMLP_CHUNK
    rows = lax.broadcasted_iota(jnp.int32, (c, c), 0)
    cols = lax.broadcasted_iota(jnp.int32, (c, c), 1)
    causal = cols <= rows
    for g in range(GMLP_GROUPS):
        wg = jnp.where(causal, w_ref[g], 0.0).astype(BF16)
        bcol = bt_ref[:, g:g + 1]
        for n in range(n_chunks):
            vb = v_ref[n * c:(n + 1) * c, g * c:(g + 1) * c]
            mixed = _dot(wg, vb) + bcol
            ub = u_ref[n * c:(n + 1) * c, g * c:(g + 1) * c].astype(F32)
            y_ref[n * c:(n + 1) * c, g * c:(g + 1) * c] = (ub * mixed).astype(BF16)


def _xattn_tile(q_ref, kt_ref, v_ref, y_ref, *, hd):
    for h in range(XA_HEADS):
        cs = slice(h * hd, (h + 1) * hd)
        sc = _dot(q_ref[:, cs], kt_ref[0, cs, :])
        m = jnp.max(sc, axis=-1, keepdims=True)
        e = jnp.exp(sc - m)
        l = jnp.sum(e, axis=-1, keepdims=True)
        o = _dot(e.astype(BF16), v_ref[0, :, cs])
        y_ref[:, cs] = (o / l).astype(BF16)


def _first_argmax(x, rowf, n):
    m = jnp.max(x, axis=0, keepdims=True)
    first = jnp.min(jnp.where(x == m, rowf, float(n)), axis=0, keepdims=True)
    return m, first


def _strict_lower(n, m, transpose=False):
    r = lax.broadcasted_iota(jnp.int32, (n, m), 0)
    c = lax.broadcasted_iota(jnp.int32, (n, m), 1)
    return jnp.where((r < c) if transpose else (c < r), 1.0, 0.0).astype(BF16)


def _route_tile(lt, rb_ref, run_ref):
    tr = lt.shape[1]
    gsz = N_EXPERTS // N_GROUPS
    s = _sigmoid(lt)
    choice = s + rb_ref[:, 0:1]
    neg_inf = -jnp.inf

    rowg = lax.broadcasted_iota(jnp.int32, (gsz, tr), 0).astype(F32)
    gs_rows = []
    for g in range(N_GROUPS):
        cg = choice[g * gsz:(g + 1) * gsz, :]
        m1, f1 = _first_argmax(cg, rowg, gsz)
        m2 = jnp.max(jnp.where(rowg == f1, neg_inf, cg), axis=0, keepdims=True)
        gs_rows.append(m1 + m2)
    gscore = jnp.concatenate(gs_rows, axis=0)

    rowG = lax.broadcasted_iota(jnp.int32, (N_GROUPS, tr), 0).astype(F32)
    gmask = jnp.zeros((N_GROUPS, tr), F32)
    for _ in range(TOPK_GROUPS):
        _, f = _first_argmax(gscore, rowG, N_GROUPS)
        hit = rowG == f
        gmask = jnp.where(hit, 1.0, gmask)
        gscore = jnp.where(hit, neg_inf, gscore)
    emask = jnp.concatenate(
        [jnp.broadcast_to(gmask[g:g + 1, :], (gsz, tr)) for g in range(N_GROUPS)], axis=0)

    rowE = lax.broadcasted_iota(jnp.int32, (N_EXPERTS, tr), 0).astype(F32)
    masked = jnp.where(emask > 0.5, choice, neg_inf)
    hits = []
    self32 = jnp.zeros((N_EXPERTS, tr), F32)
    for _ in range(TOP_K):
        _, f = _first_argmax(masked, rowE, N_EXPERTS)
        hit = rowE == f
        hits.append(hit)
        self32 = jnp.where(hit, 1.0, self32)
        masked = jnp.where(hit, neg_inf, masked)

    wsel = self32 * s
    denom = jnp.sum(wsel, axis=0, keepdims=True)
    w = wsel / denom * ROUTED_SCALE

    rank = _dot(self32.astype(BF16), _strict_lower(tr, tr, transpose=True))
    n_e = jnp.sum(self32, axis=1, keepdims=True)
    run_len = jnp.floor((n_e + (RUN_ALIGN - 1)) * (1.0 / RUN_ALIGN)) * RUN_ALIGN
    run_len_b = jnp.broadcast_to(run_len, (N_EXPERTS, 128))
    run_start = _dot(_strict_lower(N_EXPERTS, N_EXPERTS), run_len_b.astype(BF16))[:, 0:1]
    pos = rank + run_start

    run_off = run_ref[...].astype(jnp.int32)
    run_ref[...] = run_ref[...] + run_len_b
    wk = jnp.concatenate([jnp.sum(jnp.where(hit, w, 0.0), axis=0, keepdims=True) for hit in hits], axis=0)
    pk = jnp.concatenate([jnp.sum(jnp.where(hit, pos, 0.0), axis=0, keepdims=True) for hit in hits], axis=0)
    return pk.astype(jnp.int32), wk, run_len_b.astype(jnp.int32), run_off


def _merge_kernel(x_ref, lg_ref, lb_ref, yg_ref, yl_ref, yx_ref, g0_ref, g1_ref, g2_ref,
                  w0_ref, w1_ref, w2_ref, wo_ref, l1g_ref, l1b_ref, wrt_ref, rb_ref,
                  h1_ref, pos_ref, wk_ref, len_ref, off_ref, run_ref):
    @pl.when(pl.program_id(0) == 0)
    def _():
        run_ref[...] = jnp.zeros(run_ref.shape, F32)

    h0 = _ln(x_ref[...], lg_ref[...], lb_ref[...])
    merged = (g0_ref[...].astype(F32) * _dot(yg_ref[...], w0_ref[...])
              + g1_ref[...].astype(F32) * _dot(yl_ref[...], w1_ref[...])
              + g2_ref[...].astype(F32) * _dot(yx_ref[...], w2_ref[...]))
    h1 = _ln(DN_ALPHA * h0 + _dot(merged.astype(BF16), wo_ref[...]), l1g_ref[...], l1b_ref[...])
    h1_ref[...] = h1
    lt = lax.dot_general(wrt_ref[...], h1, (((1,), (1,)), ((), ())),
                         precision=lax.Precision.HIGHEST, preferred_element_type=F32)
    for sub in range(h1.shape[0] // ROUTE_TILE):
        cols = slice(sub * ROUTE_TILE, (sub + 1) * ROUTE_TILE)
        pos, wk, run_len, run_off = _route_tile(lt[:, cols], rb_ref, run_ref)
        pos_ref[:, cols] = pos
        wk_ref[:, cols] = wk
        len_ref[sub] = run_len
        off_ref[sub] = run_off


def _merge(x2, ln_g, ln_b, yg, yl, yx, g0, g1, g2, w0, w1, w2, wo, l1g, l1b, wrt, rb, tm):
    t, d = x2.shape
    e = wrt.shape[0]
    sub = tm // ROUTE_TILE
    row = pl.BlockSpec((tm, d), lambda i: (i, 0))
    vec = pl.BlockSpec((1, d), lambda i: (0, 0))
    wsq = pl.BlockSpec((d, d), lambda i: (0, 0))
    kblk = pl.BlockSpec((TOP_K, tm), lambda i: (0, i))
    tblk = pl.BlockSpec((sub, e, 128), lambda i: (i, 0, 0))
    return pl.pallas_call(
        _merge_kernel,
        grid=(t // tm,),
        in_specs=[row, vec, vec, row, row, row, row, row, row, wsq, wsq, wsq, wsq, vec, vec,
                  pl.BlockSpec(wrt.shape, lambda i: (0, 0)), pl.BlockSpec(rb.shape, lambda i: (0, 0))],
        out_specs=[row, kblk, kblk, tblk, tblk],
        out_shape=[jax.ShapeDtypeStruct((t, d), F32),
                   jax.ShapeDtypeStruct((TOP_K, t), jnp.int32),
                   jax.ShapeDtypeStruct((TOP_K, t), F32),
                   jax.ShapeDtypeStruct((t // ROUTE_TILE, e, 128), jnp.int32),
                   jax.ShapeDtypeStruct((t // ROUTE_TILE, e, 128), jnp.int32)],
        scratch_shapes=[pltpu.VMEM((e, 128), F32)],
        compiler_params=_cparams(("arbitrary",)),
        name="merge",
    )(x2, ln_g, ln_b, yg, yl, yx, g0, g1, g2, w0, w1, w2, wo, l1g, l1b, wrt, rb)


def _run_copy(vmem_buf, hbm_ref, off, row, n, sem, to_hbm):
    n = pl.multiple_of(n, RUN_ALIGN)
    v = vmem_buf.at[pl.ds(pl.multiple_of(off, RUN_ALIGN), n), :]
    h = hbm_ref.at[pl.ds(pl.multiple_of(row, RUN_ALIGN), n), :]
    return pltpu.make_async_copy(v, h, sem) if to_hbm else pltpu.make_async_copy(h, v, sem)


def _start_runs(len_ref, row_ref, tile, vmem_buf, hbm_ref, sem, *, to_hbm):
    def body(i, off):
        for u in range(RUN_ISSUE_UNROLL):
            e = i * RUN_ISSUE_UNROLL + u
            n = len_ref[tile * N_EXPERTS + e]

            @pl.when(n > 0)
            def _():
                _run_copy(vmem_buf, hbm_ref, off, row_ref[tile * N_EXPERTS + e], n, sem, to_hbm).start()
            off = off + n
        return off

    lax.fori_loop(0, N_EXPERTS // RUN_ISSUE_UNROLL, body, jnp.int32(0))


def _wait_runs(rows_ref, tile, vmem_buf, hbm_ref, sem, *, to_hbm):
    n = rows_ref[tile]

    @pl.when(n > 0)
    def _():
        _run_copy(vmem_buf, hbm_ref, 0, 0, n, sem, to_hbm).wait()


def _chunks_always(tr):
    return (tr * TOP_K + N_EXPERTS * RUN_ALIGN // 2) // SORT_CHUNK


def _assignment_rows(pos_ref, val_ref, chunk, n_rows):
    tr = pos_ref.shape[1]
    rows = lax.broadcasted_iota(jnp.int32, (n_rows, tr), 0).astype(F32).astype(BF16)
    m = jnp.zeros((n_rows, tr), BF16)
    for k in range(TOP_K):
        rel = (pos_ref[k:k + 1, :] - chunk * n_rows).astype(F32).astype(BF16)
        v = jnp.ones((1, tr), BF16) if val_ref is None else val_ref[k:k + 1, :].astype(BF16)
        m = jnp.where(rows == rel, v, m)
    return m


def _dispatch_kernel(len_ref, row_ref, rows_ref, zb_ref, pos_ref, h1_ref, xs_ref, buf, zbuf, h1b, sem, zsem, *, rs):
    tile = pl.program_id(0)
    n_tiles = pl.num_programs(0)
    slot = tile % 2

    def zero_fill(group, lo, hi, *, wait):
        def zcopy(j):
            return pltpu.make_async_copy(
                zbuf, xs_ref.at[pl.ds(zb_ref[j] * EXPERT_BLOCK, EXPERT_BLOCK), :], zsem.at[group])

        def body(j, carry):
            @pl.when(zb_ref[j] >= 0)
            def _():
                zcopy(j).wait() if wait else zcopy(j).start()
            return carry

        lax.fori_loop(lo, hi, body, 0)

    @pl.when(tile == 0)
    def _():
        zbuf[...] = jnp.zeros(zbuf.shape, zbuf.dtype)
        zero_fill(0, 0, N_EXPERTS, wait=False)
        zero_fill(1, N_EXPERTS, zb_ref.shape[0], wait=False)
        zero_fill(0, 0, N_EXPERTS, wait=True)

    @pl.when(tile >= 2)
    def _():
        _wait_runs(rows_ref, tile - 2, buf.at[slot], xs_ref, sem.at[slot], to_hbm=True)

    h1b[...] = h1_ref[...].astype(BF16)

    def sort_chunk(c):
        onehot = _assignment_rows(pos_ref, None, c, SORT_CHUNK)
        srt = _dot(onehot, h1b[...])
        buf[slot, c * SORT_CHUNK:(c + 1) * SORT_CHUNK, :] = srt.astype(BF16)

    n_always = _chunks_always(h1_ref.shape[0])
    for c in range(n_always):
        sort_chunk(c)
    for c in range(n_always, rs // SORT_CHUNK):
        pl.when(rows_ref[tile] > c * SORT_CHUNK)(functools.partial(sort_chunk, c))
    _start_runs(len_ref, row_ref, tile, buf.at[slot], xs_ref, sem.at[slot], to_hbm=True)

    @pl.when(tile == n_tiles - 1)
    def _():
        @pl.when(tile >= 1)
        def _():
            _wait_runs(rows_ref, tile - 1, buf.at[1 - slot], xs_ref, sem.at[1 - slot], to_hbm=True)
        _wait_runs(rows_ref, tile, buf.at[slot], xs_ref, sem.at[slot], to_hbm=True)
        zero_fill(1, N_EXPERTS, zb_ref.shape[0], wait=True)


def _dispatch(run_len, run_row, tile_rows, zero_blocks, pos, h1, n_rows, tr, rs):
    t, d = h1.shape
    return pl.pallas_call(
        functools.partial(_dispatch_kernel, rs=rs),
        grid_spec=pltpu.PrefetchScalarGridSpec(
            num_scalar_prefetch=4, grid=(t // tr,),
            in_specs=[pl.BlockSpec((TOP_K, tr), lambda i, *_: (0, i)),
                      pl.BlockSpec((tr, d), lambda i, *_: (i, 0))],
            out_specs=pl.BlockSpec(memory_space=pl.ANY),
            scratch_shapes=[pltpu.VMEM((2, rs, d), BF16),
                            pltpu.VMEM((EXPERT_BLOCK, d), BF16),
                            pltpu.VMEM((tr, d), BF16),
                            pltpu.SemaphoreType.DMA((2,)), pltpu.SemaphoreType.DMA((2,))]),
        out_shape=jax.ShapeDtypeStruct((n_rows, d), BF16),
        compiler_params=_cparams(("arbitrary",)),
        name="dispatch",
    )(run_len, run_row, tile_rows, zero_blocks, pos, h1)


def _expert_kernel(be_ref, nu_ref, xs_ref, wg_ref, wu_ref, wd_ref, ys_ref, wgu_s, wd_s, *, ed):
    i = pl.program_id(0)
    live = i < nu_ref[0]
    new_expert = jnp.logical_or(i == 0, be_ref[i] != be_ref[jnp.maximum(i - 1, 0)])

    @pl.when(jnp.logical_and(live, new_expert))
    def _():
        wgu_s[:, :ed] = wg_ref[0].astype(BF16)
        wgu_s[:, ed:] = wu_ref[0].astype(BF16)
        wd_s[...] = wd_ref[0].astype(BF16)

    @pl.when(live)
    def _():
        gu = _dot(xs_ref[...], wgu_s[...])
        g = gu[:, :ed]
        act = (g * _sigmoid(g) * gu[:, ed:]).astype(BF16)
        ys_ref[...] = _dot(act, wd_s[...]).astype(BF16)

    @pl.when(jnp.logical_not(live))
    def _():
        ys_ref[...] = jnp.zeros(ys_ref.shape, ys_ref.dtype)


def _experts(block_e, n_used, xs, wg, wu, wd):
    n_rows, d = xs.shape
    n_blocks = n_rows // EXPERT_BLOCK
    ed = wg.shape[2]

    def xmap(i, be, nu):
        return (jnp.minimum(i, nu[0] - 1), 0)

    def ymap(i, be, nu):
        return (i, 0)

    def wmap(i, be, nu):
        return (be[i], 0, 0)

    return pl.pallas_call(
        functools.partial(_expert_kernel, ed=ed),
        grid_spec=pltpu.PrefetchScalarGridSpec(
            num_scalar_prefetch=2, grid=(n_blocks,),
            in_specs=[pl.BlockSpec((EXPERT_BLOCK, d), xmap),
                      pl.BlockSpec((1, d, ed), wmap),
                      pl.BlockSpec((1, d, ed), wmap),
                      pl.BlockSpec((1, ed, d), wmap)],
            out_specs=pl.BlockSpec((EXPERT_BLOCK, d), ymap),
            scratch_shapes=[pltpu.VMEM((d, 2 * ed), BF16), pltpu.VMEM((ed, d), BF16)]),
        out_shape=jax.ShapeDtypeStruct((n_rows, d), BF16),
        compiler_params=_cparams(("arbitrary",)),
        name="experts",
    )(block_e, n_used, xs, wg, wu, wd)


def _combine_kernel(len_ref, row_ref, rows_ref, h1_ref, pos_ref, wk_ref, ys_ref, sgu_ref, sd_ref,
                    l2g_ref, l2b_ref, out_ref, ybuf, wmt, sem, *, rs, sdim):
    tile = pl.program_id(0)
    n_tiles = pl.num_programs(0)
    slot = tile % 2

    @pl.when(tile == 0)
    def _():
        ybuf[...] = jnp.zeros(ybuf.shape, ybuf.dtype)
        _start_runs(len_ref, row_ref, tile, ybuf.at[0], ys_ref, sem.at[0], to_hbm=False)

    @pl.when(tile + 1 < n_tiles)
    def _():
        _start_runs(len_ref, row_ref, tile + 1, ybuf.at[1 - slot], ys_ref, sem.at[1 - slot], to_hbm=False)

    _wait_runs(rows_ref, tile, ybuf.at[slot], ys_ref, sem.at[slot], to_hbm=False)

    h1 = h1_ref[...]
    gu = _dot(h1.astype(BF16), sgu_ref[...])
    g = gu[:, :sdim]
    act = (g * _sigmoid(g) * gu[:, sdim:]).astype(BF16)
    acc = DN_ALPHA * h1 + _dot(act, sd_ref[...])

    def weights_chunk(c):
        wmt[c * SORT_CHUNK:(c + 1) * SORT_CHUNK, :] = _assignment_rows(pos_ref, wk_ref, c, SORT_CHUNK)

    def unsort(lo, hi):
        return lax.dot_general(wmt[lo:hi, :], ybuf[slot, lo:hi, :], (((0,), (0,)), ((), ())),
                               preferred_element_type=F32)

    n_always = _chunks_always(h1.shape[0])
    for c in range(n_always):
        weights_chunk(c)
    out_ref[...] = acc + unsort(0, n_always * SORT_CHUNK)
    for c in range(n_always, rs // SORT_CHUNK):
        @pl.when(rows_ref[tile] > c * SORT_CHUNK)
        def _(c=c):
            weights_chunk(c)
            out_ref[...] += unsort(c * SORT_CHUNK, (c + 1) * SORT_CHUNK)
    out_ref[...] = _ln(out_ref[...], l2g_ref[...], l2b_ref[...])


def _combine(run_len, run_row, tile_rows, h1, pos, wk, ys, sgu, sd, l2g, l2b, tr, rs):
    t, d = h1.shape
    sdim = sd.shape[0]
    row = pl.BlockSpec((tr, d), lambda i, *_: (i, 0))
    krow = pl.BlockSpec((TOP_K, tr), lambda i, *_: (0, i))
    vec = pl.BlockSpec((1, d), lambda i, *_: (0, 0))
    return pl.pallas_call(
        functools.partial(_combine_kernel, rs=rs, sdim=sdim),
        grid_spec=pltpu.PrefetchScalarGridSpec(
            num_scalar_prefetch=3, grid=(t // tr,),
            in_specs=[row, krow, krow,
                      pl.BlockSpec(memory_space=pl.ANY),
                      pl.BlockSpec(sgu.shape, lambda i, *_: (0, 0)),
                      pl.BlockSpec(sd.shape, lambda i, *_: (0, 0)),
                      vec, vec],
            out_specs=row,
            scratch_shapes=[pltpu.VMEM((2, rs, d), BF16),
                            pltpu.VMEM((rs, tr), BF16),
                            pltpu.SemaphoreType.DMA((2,))]),
        out_shape=jax.ShapeDtypeStruct((t, d), F32),
        compiler_params=_cparams(("arbitrary",)),
        name="combine",
    )(run_len, run_row, tile_rows, h1, pos, wk, ys, sgu, sd, l2g, l2b)


def _tile(n, pref):
    while n % pref:
        pref //= 2
    return pref


def kernel(x, mem, ln_in_g, ln_in_b, w_in, gmlp_norm_g, gmlp_norm_b, gmlp_spatial_w, gmlp_spatial_b, conv_w, conv_b, lru_wa, lru_ba, lru_wi, lru_bi, lru_lambda, w_kv, w_br_gmlp, w_br_lru, w_br_xa, w_out, ln1_g, ln1_b, w_router, router_bias, exp_gate, exp_up, exp_down, sh_gate, sh_up, sh_down, ln2_g, ln2_b):
    b, s, d = x.shape
    t = b * s
    assert w_in.shape[0] == DEPTH and w_in.shape[2] == 8 * d and s % GMLP_CHUNK == 0
    r2 = lambda a: a.reshape(1, -1)
    x2 = x.reshape(t, d)
    lg, lb = r2(ln_in_g), r2(ln_in_b)
    hd = d // XA_HEADS

    wkv = w_kv[0]
    kt, vv = _kv(mem, wkv[:, :d].T.astype(BF16), wkv[:, d:].astype(BF16))

    tm = _tile(s, 512)
    assert tm % GMLP_CHUNK == 0
    wai = jnp.concatenate([lru_wa[0], lru_wi[0]], axis=-1).astype(BF16)
    yg, yl, yx, g0, g1, g2 = _inproj(
        x2, lg, lb, w_in[0].astype(BF16), r2(gmlp_norm_g[0]), r2(gmlp_norm_b[0]),
        conv_w[0], r2(conv_b[0]), wai, r2(lru_ba[0]), r2(lru_bi[0]), r2(lru_lambda[0]),
        gmlp_spatial_w[0], gmlp_spatial_b[0].T, kt, vv, b, tm, hd ** -0.5)

    tr = ROUTE_TILE
    assert tm % tr == 0
    n_tiles = t // tr
    rs = tr * TOP_K + N_EXPERTS * RUN_ALIGN
    rb = jnp.broadcast_to(router_bias[0].astype(F32)[:, None], (N_EXPERTS, 128))
    h1, pos, wk, run_len3, run_off3 = _merge(
        x2, lg, lb, yg, yl, yx, g0, g1, g2,
        w_br_gmlp[0].astype(BF16), w_br_lru[0].astype(BF16), w_br_xa[0].astype(BF16),
        w_out[0].astype(BF16), r2(ln1_g[0]), r2(ln1_b[0]), w_router[0].T, rb, tm)

    run_len = run_len3[:, :, 0]
    run_off = run_off3[:, :, 0]
    total = run_off[-1] + run_len[-1]
    seg_len = (total + EXPERT_BLOCK - 1) // EXPERT_BLOCK * EXPERT_BLOCK
    seg_end = jnp.cumsum(seg_len)
    seg_start = seg_end - seg_len
    run_row = seg_start[None, :] + run_off
    tile_rows = jnp.sum(run_len, axis=1).astype(jnp.int32)
    max_rows = t * TOP_K + n_tiles * N_EXPERTS * (RUN_ALIGN - 1)
    n_blocks = -(-max_rows // EXPERT_BLOCK) + N_EXPERTS
    block_start = jnp.arange(n_blocks, dtype=jnp.int32) * EXPERT_BLOCK
    block_e = jnp.minimum(jnp.sum(block_start[:, None] >= seg_end[None, :], axis=1),
                          N_EXPERTS - 1).astype(jnp.int32)
    n_used = (seg_end[-1:] // EXPERT_BLOCK).astype(jnp.int32)
    n_tail = n_blocks - (t * TOP_K) // EXPERT_BLOCK
    tail_blk = n_used[0] + jnp.arange(n_tail, dtype=jnp.int32)
    zero_blocks = jnp.concatenate([
        jnp.where(total % EXPERT_BLOCK != 0, seg_end // EXPERT_BLOCK - 1, -1),
        jnp.where(tail_blk < n_blocks, tail_blk, -1)]).astype(jnp.int32)

    run_len_flat = run_len.reshape(-1).astype(jnp.int32)
    run_row_flat = run_row.reshape(-1).astype(jnp.int32)
    xs = _dispatch(run_len_flat, run_row_flat, tile_rows, zero_blocks, pos, h1,
                   n_blocks * EXPERT_BLOCK, tr, rs)
    ys = _experts(block_e, n_used, xs, exp_gate[0], exp_up[0], exp_down[0])
    sgu = jnp.concatenate([sh_gate[0], sh_up[0]], axis=-1).astype(BF16)
    out = _combine(run_len_flat, run_row_flat, tile_rows, h1, pos, wk, ys, sgu, sh_down[0].astype(BF16),
                   r2(ln2_g[0]), r2(ln2_b[0]), tr, rs)
    return out.reshape(b, s, d)
```

```python
import functools

import jax
import jax.numpy as jnp
from jax import lax
from jax.experimental import pallas as pl
from jax.experimental.pallas import tpu as pltpu

F32 = jnp.float32
BF16 = jnp.bfloat16

GMLP_GROUPS = 8
GMLP_CHUNK = 128
LRU_BLOCKS = 8
CONV_WIDTH = 4
LRU_C = 8.0
XA_HEADS = 4
N_EXPERTS = 64
TOP_K = 8
N_GROUPS = 8
TOPK_GROUPS = 4
ROUTED_SCALE = 2.5
EXPERT_BLOCK = 1024
ROUTE_TILE = 256
RUN_ALIGN = 16
SORT_CHUNK = 256
RUN_ISSUE_UNROLL = 4
SCAN_GROUP = 8
LRU_ROWS = 128
LN_EPS = 1e-5
DEPTH = 1
DN_ALPHA = (2.0 * DEPTH) ** 0.25

VMEM_LIMIT_V7X = 56 * 1024 * 1024


def _cparams(sem):
    return pltpu.CompilerParams(dimension_semantics=sem, vmem_limit_bytes=VMEM_LIMIT_V7X)


def _ln(x, g, b):
    mu = jnp.mean(x, axis=-1, keepdims=True)
    xc = x - mu
    var = jnp.mean(xc * xc, axis=-1, keepdims=True)
    return xc * lax.rsqrt(var + LN_EPS) * g + b


def _gelu(x):
    return x * (0.5 * (1.0 + jnp.tanh(0.7978845608028654 * (x + 0.044715 * (x * x * x)))))


def _sigmoid(x):
    return 0.5 * (1.0 + jnp.tanh(0.5 * x))


def _dot(a, b):
    return jnp.dot(a, b, preferred_element_type=F32)


def _lru_block(blk, r0, xe_ref, gate_ref, cw_ref, cb_ref, wai_ref, ba_ref, bi_ref, lam_ref, hc_ref, y_ref,
               *, nr, bd):
    cs = slice(blk * bd, (blk + 1) * bd)
    xc = cb_ref[:, cs] + jnp.zeros((nr, bd), F32)
    for k in range(CONV_WIDTH):
        xc = xc + cw_ref[k:k + 1, cs] * xe_ref[8 + r0 - k:8 + r0 - k + nr, cs]
    ri = _dot(xc.astype(BF16), wai_ref[blk])
    r = _sigmoid(ri[:, :bd] + ba_ref[:, cs])
    ig = _sigmoid(ri[:, bd:] + bi_ref[:, cs])
    lam = lam_ref[:, cs]
    softplus_neg = jnp.maximum(-lam, 0.0) + jnp.log1p(jnp.exp(-jnp.abs(lam)))
    log_a = (-LRU_C) * r * softplus_neg
    a = jnp.exp(log_a)
    om = 1.0 - a * a
    mult = jnp.where(om > 0.0, om * lax.rsqrt(om), 0.0)
    b = mult * (ig * xc)
    ng = nr // SCAN_GROUP
    a = a.reshape(ng, SCAN_GROUP, bd)
    b = b.reshape(ng, SCAN_GROUP, bd)
    row_in_group = lax.broadcasted_iota(jnp.int32, (ng, SCAN_GROUP, bd), 1)
    dsh = 1
    while dsh < SCAN_GROUP:
        a_sh = pltpu.roll(a, dsh, 1)
        b_sh = pltpu.roll(b, dsh, 1)
        keep = row_in_group >= dsh
        b = jnp.where(keep, a * b_sh + b, b)
        a = jnp.where(keep, a * a_sh, a)
        dsh *= 2
    carry = hc_ref[0:1, cs]
    hs = []
    for g in range(ng):
        hs.append(b[g] + a[g] * carry)
        carry = hs[-1][SCAN_GROUP - 1:SCAN_GROUP, :]
    hc_ref[0:1, cs] = carry
    y_ref[r0:r0 + nr, cs] = (jnp.concatenate(hs, axis=0) * gate_ref[r0:r0 + nr, cs]).astype(BF16)


def _inproj_kernel(x_ref, lg_ref, lb_ref, w_ref, ng_ref, nb_ref, cw_ref, cb_ref, wai_ref, ba_ref, bi_ref, lam_ref,
                   wsp_ref, bspt_ref, kt_ref, vv_ref,
                   yg_ref, yl_ref, yx_ref, g0_ref, g1_ref, g2_ref,
                   xe_ref, hc_ref, gate_ref, u_ref, v_ref, q_ref, *, d, q_scale):
    @pl.when(pl.program_id(1) == 0)
    def _():
        xe_ref[0:8, :] = jnp.zeros((8, d), F32)
        hc_ref[...] = jnp.zeros(hc_ref.shape, F32)

    tm = x_ref.shape[0]
    hb = _ln(x_ref[...], lg_ref[...], lb_ref[...]).astype(BF16)

    def z(j):
        return _dot(hb, w_ref[:, j * d:(j + 1) * d])

    def lru(blk):
        for r0 in range(0, tm, LRU_ROWS):
            _lru_block(blk, r0, xe_ref, gate_ref, cw_ref, cb_ref, wai_ref, ba_ref, bi_ref, lam_ref, hc_ref,
                       yl_ref, nr=LRU_ROWS, bd=d // LRU_BLOCKS)

    xe_ref[8:8 + tm, :] = z(2)
    gate_ref[...] = _gelu(z(3))
    u_ref[...] = _gelu(z(0)).astype(BF16)
    lru(0), lru(1)
    v_ref[...] = _ln(_gelu(z(1)), ng_ref[...], nb_ref[...]).astype(BF16)
    lru(2), lru(3)
    q_ref[...] = (z(4) * q_scale).astype(BF16)
    lru(4)
    g0_ref[...] = _sigmoid(z(5)).astype(BF16)
    lru(5)
    g1_ref[...] = _sigmoid(z(6)).astype(BF16)
    lru(6)
    g2_ref[...] = _sigmoid(z(7)).astype(BF16)
    lru(7)
    xe_ref[0:8, :] = xe_ref[tm:tm + 8, :]
    _gmlp_tile(u_ref, v_ref, wsp_ref, bspt_ref, yg_ref, n_chunks=tm // GMLP_CHUNK)
    _xattn_tile(q_ref, kt_ref, vv_ref, yx_ref, hd=d // XA_HEADS)


def _inproj(x2, ln_g, ln_b, w_in, ng, nb, conv_w, conv_b, wai, ba, bi, lam, w_sp, b_sp_t, kt, vv, batch, tm,
            q_scale):
    t, d = x2.shape
    n_s = t // batch // tm
    n_out = 6
    row = pl.BlockSpec((tm, d), lambda b, s: (b * n_s + s, 0))
    vec = pl.BlockSpec((1, d), lambda b, s: (0, 0))
    wspec = pl.BlockSpec(w_in.shape, lambda b, s: (0, 0), pipeline_mode=pl.Buffered(1))
    return pl.pallas_call(
        functools.partial(_inproj_kernel, d=d, q_scale=q_scale),
        grid=(batch, n_s),
        in_specs=[row, vec, vec, wspec, vec, vec,
                  pl.BlockSpec(conv_w.shape, lambda b, s: (0, 0)), vec,
                  pl.BlockSpec(wai.shape, lambda b, s: (0, 0, 0)), vec, vec, vec,
                  pl.BlockSpec(w_sp.shape, lambda b, s: (0, 0, 0)),
                  pl.BlockSpec(b_sp_t.shape, lambda b, s: (0, 0)),
                  pl.BlockSpec((1,) + kt.shape[1:], lambda b, s: (b, 0, 0)),
                  pl.BlockSpec((1,) + vv.shape[1:], lambda b, s: (b, 0, 0))],
        out_specs=[row] * n_out,
        out_shape=[jax.ShapeDtypeStruct((t, d), BF16)] * n_out,
        scratch_shapes=[pltpu.VMEM((tm + 8, d), F32), pltpu.VMEM((8, d), F32), pltpu.VMEM((tm, d), F32),
                        pltpu.VMEM((tm, d), BF16), pltpu.VMEM((tm, d), BF16), pltpu.VMEM((tm, d), BF16)],
        compiler_params=_cparams(("arbitrary", "arbitrary")),
        name="inproj",
    )(x2, ln_g, ln_b, w_in, ng, nb, conv_w, conv_b, wai, ba, bi, lam, w_sp, b_sp_t, kt, vv)


def _kv_kernel(mem_ref, wkt_ref, wv_ref, kt_ref, v_ref):
    m = mem_ref[0].astype(BF16)
    kt = lax.dot_general(wkt_ref[...], m, (((1,), (1,)), ((), ())), preferred_element_type=F32)
    kt_ref[0] = kt.astype(BF16)
    v_ref[0] = _dot(m, wv_ref[...]).astype(BF16)


def _kv(mem, wkt, wv):
    b, m, d = mem.shape
    xw = wv.shape[1]
    return pl.pallas_call(
        _kv_kernel,
        grid=(b,),
        in_specs=[pl.BlockSpec((1, m, d), lambda i: (i, 0, 0)),
                  pl.BlockSpec(wkt.shape, lambda i: (0, 0)),
                  pl.BlockSpec(wv.shape, lambda i: (0, 0))],
        out_specs=[pl.BlockSpec((1, xw, m), lambda i: (i, 0, 0)),
                   pl.BlockSpec((1, m, xw), lambda i: (i, 0, 0))],
        out_shape=[jax.ShapeDtypeStruct((b, xw, m), BF16), jax.ShapeDtypeStruct((b, m, xw), BF16)],
        compiler_params=_cparams(("parallel",)),
        name="kv",
    )(mem, wkt, wv)


def _gmlp_tile(u_ref, v_ref, w_ref, bt_ref, y_ref, *, n_chunks):
    c = GMLP_CHUNK
    rows = lax.broadcasted_iota(jnp.int32, (c, c), 0)
    cols = lax.broadcasted_iota(jnp.int32, (c, c), 1)
    causal = cols <= rows
    for g in range(GMLP_GROUPS):
        wg = jnp.where(causal, w_ref[g], 0.0).astype(BF16)
        bcol = bt_ref[:, g:g + 1]
        for n in range(n_chunks):
            vb = v_ref[n * c:(n + 1) * c, g * c:(g + 1) * c]
            mixed = _dot(wg, vb) + bcol
            ub = u_ref[n * c:(n + 1) * c, g * c:(g + 1) * c].astype(F32)
            y_ref[n * c:(n + 1) * c, g * c:(g + 1) * c] = (ub * mixed).astype(BF16)


def _xattn_tile(q_ref, kt_ref, v_ref, y_ref, *, hd):
    for h in range(XA_HEADS):
        cs = slice(h * hd, (h + 1) * hd)
        sc = _dot(q_ref[:, cs], kt_ref[0, cs, :])
        m = jnp.max(sc, axis=-1, keepdims=True)
        e = jnp.exp(sc - m)
        l = jnp.sum(e, axis=-1, keepdims=True)
        o = _dot(e.astype(BF16), v_ref[0, :, cs])
        y_ref[:, cs] = (o / l).astype(BF16)


def _first_argmax(x, rowf, n):
    m = jnp.max(x, axis=0, keepdims=True)
    first = jnp.min(jnp.where(x == m, rowf, float(n)), axis=0, keepdims=True)
    return m, first


def _strict_lower(n, m, transpose=False):
    r = lax.broadcasted_iota(jnp.int32, (n, m), 0)
    c = lax.broadcasted_iota(jnp.int32, (n, m), 1)
    return jnp.where((r < c) if transpose else (c < r), 1.0, 0.0).astype(BF16)


def _route_tile(lt, rb_ref, run_ref):
    tr = lt.shape[1]
    gsz = N_EXPERTS // N_GROUPS
    s = _sigmoid(lt)
    choice = s + rb_ref[:, 0:1]
    neg_inf = -jnp.inf

    rowg = lax.broadcasted_iota(jnp.int32, (gsz, tr), 0).astype(F32)
    gs_rows = []
    for g in range(N_GROUPS):
        cg = choice[g * gsz:(g + 1) * gsz, :]
        m1, f1 = _first_argmax(cg, rowg, gsz)
        m2 = jnp.max(jnp.where(rowg == f1, neg_inf, cg), axis=0, keepdims=True)
        gs_rows.append(m1 + m2)
    gscore = jnp.concatenate(gs_rows, axis=0)

    rowG = lax.broadcasted_iota(jnp.int32, (N_GROUPS, tr), 0).astype(F32)
    gmask = jnp.zeros((N_GROUPS, tr), F32)
    for _ in range(TOPK_GROUPS):
        _, f = _first_argmax(gscore, rowG, N_GROUPS)
        hit = rowG == f
        gmask = jnp.where(hit, 1.0, gmask)
        gscore = jnp.where(hit, neg_inf, gscore)
    emask = jnp.concatenate(
        [jnp.broadcast_to(gmask[g:g + 1, :], (gsz, tr)) for g in range(N_GROUPS)], axis=0)

    rowE = lax.broadcasted_iota(jnp.int32, (N_EXPERTS, tr), 0).astype(F32)
    masked = jnp.where(emask > 0.5, choice, neg_inf)
    hits = []
    self32 = jnp.zeros((N_EXPERTS, tr), F32)
    for _ in range(TOP_K):
        _, f = _first_argmax(masked, rowE, N_EXPERTS)
        hit = rowE == f
        hits.append(hit)
        self32 = jnp.where(hit, 1.0, self32)
        masked = jnp.where(hit, neg_inf, masked)

    wsel = self32 * s
    denom = jnp.sum(wsel, axis=0, keepdims=True)
    w = wsel / denom * ROUTED_SCALE

    rank = _dot(self32.astype(BF16), _strict_lower(tr, tr, transpose=True))
    n_e = jnp.sum(self32, axis=1, keepdims=True)
    run_len = jnp.floor((n_e + (RUN_ALIGN - 1)) * (1.0 / RUN_ALIGN)) * RUN_ALIGN
    run_len_b = jnp.broadcast_to(run_len, (N_EXPERTS, 128))
    run_start = _dot(_strict_lower(N_EXPERTS, N_EXPERTS), run_len_b.astype(BF16))[:, 0:1]
    pos = rank + run_start

    run_off = run_ref[...].astype(jnp.int32)
    run_ref[...] = run_ref[...] + run_len_b
    wk = jnp.concatenate([jnp.sum(jnp.where(hit, w, 0.0), axis=0, keepdims=True) for hit in hits], axis=0)
    pk = jnp.concatenate([jnp.sum(jnp.where(hit, pos, 0.0), axis=0, keepdims=True) for hit in hits], axis=0)
    return pk.astype(jnp.int32), wk, run_len_b.astype(jnp.int32), run_off


def _merge_kernel(x_ref, lg_ref, lb_ref, yg_ref, yl_ref, yx_ref, g0_ref, g1_ref, g2_ref,
                  w0_ref, w1_ref, w2_ref, wo_ref, l1g_ref, l1b_ref, wrt_ref, rb_ref,
                  h1_ref, pos_ref, wk_ref, len_ref, off_ref, run_ref):
    @pl.when(pl.program_id(0) == 0)
    def _():
        run_ref[...] = jnp.zeros(run_ref.shape, F32)

    h0 = _ln(x_ref[...], lg_ref[...], lb_ref[...])
    merged = (g0_ref[...].astype(F32) * _dot(yg_ref[...], w0_ref[...])
              + g1_ref[...].astype(F32) * _dot(yl_ref[...], w1_ref[...])
              + g2_ref[...].astype(F32) * _dot(yx_ref[...], w2_ref[...]))
    h1 = _ln(DN_ALPHA * h0 + _dot(merged.astype(BF16), wo_ref[...]), l1g_ref[...], l1b_ref[...])
    h1_ref[...] = h1
    lt = lax.dot_general(wrt_ref[...], h1, (((1,), (1,)), ((), ())),
                         precision=lax.Precision.HIGHEST, preferred_element_type=F32)
    for sub in range(h1.shape[0] // ROUTE_TILE):
        cols = slice(sub * ROUTE_TILE, (sub + 1) * ROUTE_TILE)
        pos, wk, run_len, run_off = _route_tile(lt[:, cols], rb_ref, run_ref)
        pos_ref[:, cols] = pos
        wk_ref[:, cols] = wk
        len_ref[sub] = run_len
        off_ref[sub] = run_off


def _merge(x2, ln_g, ln_b, yg, yl, yx, g0, g1, g2, w0, w1, w2, wo, l1g, l1b, wrt, rb, tm):
    t, d = x2.shape
    e = wrt.shape[0]
    sub = tm // ROUTE_TILE
    row = pl.BlockSpec((tm, d), lambda i: (i, 0))
    vec = pl.BlockSpec((1, d), lambda i: (0, 0))
    wsq = pl.BlockSpec((d, d), lambda i: (0, 0))
    kblk = pl.BlockSpec((TOP_K, tm), lambda i: (0, i))
    tblk = pl.BlockSpec((sub, e, 128), lambda i: (i, 0, 0))
    return pl.pallas_call(
        _merge_kernel,
        grid=(t // tm,),
        in_specs=[row, vec, vec, row, row, row, row, row, row, wsq, wsq, wsq, wsq, vec, vec,
                  pl.BlockSpec(wrt.shape, lambda i: (0, 0)), pl.BlockSpec(rb.shape, lambda i: (0, 0))],
        out_specs=[row, kblk, kblk, tblk, tblk],
        out_shape=[jax.ShapeDtypeStruct((t, d), F32),
                   jax.ShapeDtypeStruct((TOP_K, t), jnp.int32),
                   jax.ShapeDtypeStruct((TOP_K, t), F32),
                   jax.ShapeDtypeStruct((t // ROUTE_TILE, e, 128), jnp.int32),
                   jax.ShapeDtypeStruct((t // ROUTE_TILE, e, 128), jnp.int32)],
        scratch_shapes=[pltpu.VMEM((e, 128), F32)],
        compiler_params=_cparams(("arbitrary",)),
        name="merge",
    )(x2, ln_g, ln_b, yg, yl, yx, g0, g1, g2, w0, w1, w2, wo, l1g, l1b, wrt, rb)


def _run_copy(vmem_buf, hbm_ref, off, row, n, sem, to_hbm):
    n = pl.multiple_of(n, RUN_ALIGN)
    v = vmem_buf.at[pl.ds(pl.multiple_of(off, RUN_ALIGN), n), :]
    h = hbm_ref.at[pl.ds(pl.multiple_of(row, RUN_ALIGN), n), :]
    return pltpu.make_async_copy(v, h, sem) if to_hbm else pltpu.make_async_copy(h, v, sem)


def _start_runs(len_ref, row_ref, tile, vmem_buf, hbm_ref, sem, *, to_hbm):
    def body(i, off):
        for u in range(RUN_ISSUE_UNROLL):
            e = i * RUN_ISSUE_UNROLL + u
            n = len_ref[tile * N_EXPERTS + e]

            @pl.when(n > 0)
            def _():
                _run_copy(vmem_buf, hbm_ref, off, row_ref[tile * N_EXPERTS + e], n, sem, to_hbm).start()
            off = off + n
        return off

    lax.fori_loop(0, N_EXPERTS // RUN_ISSUE_UNROLL, body, jnp.int32(0))


def _wait_runs(rows_ref, tile, vmem_buf, hbm_ref, sem, *, to_hbm):
    n = rows_ref[tile]

    @pl.when(n > 0)
    def _():
        _run_copy(vmem_buf, hbm_ref, 0, 0, n, sem, to_hbm).wait()


def _chunks_always(tr):
    return (tr * TOP_K + N_EXPERTS * RUN_ALIGN // 2) // SORT_CHUNK


def _assignment_rows(pos_ref, val_ref, chunk, n_rows):
    tr = pos_ref.shape[1]
    rows = lax.broadcasted_iota(jnp.int32, (n_rows, tr), 0).astype(F32).astype(BF16)
    m = jnp.zeros((n_rows, tr), BF16)
    for k in range(TOP_K):
        rel = (pos_ref[k:k + 1, :] - chunk * n_rows).astype(F32).astype(BF16)
        v = jnp.ones((1, tr), BF16) if val_ref is None else val_ref[k:k + 1, :].astype(BF16)
        m = jnp.where(rows == rel, v, m)
    return m


def _dispatch_kernel(len_ref, row_ref, rows_ref, zb_ref, pos_ref, h1_ref, xs_ref, buf, zbuf, h1b, sem, zsem, *, rs):
    tile = pl.program_id(0)
    n_tiles = pl.num_programs(0)
    slot = tile % 2

    def zero_fill(group, lo, hi, *, wait):
        def zcopy(j):
            return pltpu.make_async_copy(
                zbuf, xs_ref.at[pl.ds(zb_ref[j] * EXPERT_BLOCK, EXPERT_BLOCK), :], zsem.at[group])

        def body(j, carry):
            @pl.when(zb_ref[j] >= 0)
            def _():
                zcopy(j).wait() if wait else zcopy(j).start()
            return carry

        lax.fori_loop(lo, hi, body, 0)

    @pl.when(tile == 0)
    def _():
        zbuf[...] = jnp.zeros(zbuf.shape, zbuf.dtype)
        zero_fill(0, 0, N_EXPERTS, wait=False)
        zero_fill(1, N_EXPERTS, zb_ref.shape[0], wait=False)
        zero_fill(0, 0, N_EXPERTS, wait=True)

    @pl.when(tile >= 2)
    def _():
        _wait_runs(rows_ref, tile - 2, buf.at[slot], xs_ref, sem.at[slot], to_hbm=True)

    h1b[...] = h1_ref[...].astype(BF16)

    def sort_chunk(c):
        onehot = _assignment_rows(pos_ref, None, c, SORT_CHUNK)
        srt = _dot(onehot, h1b[...])
        buf[slot, c * SORT_CHUNK:(c + 1) * SORT_CHUNK, :] = srt.astype(BF16)

    n_always = _chunks_always(h1_ref.shape[0])
    for c in range(n_always):
        sort_chunk(c)
    for c in range(n_always, rs // SORT_CHUNK):
        pl.when(rows_ref[tile] > c * SORT_CHUNK)(functools.partial(sort_chunk, c))
    _start_runs(len_ref, row_ref, tile, buf.at[slot], xs_ref, sem.at[slot], to_hbm=True)

    @pl.when(tile == n_tiles - 1)
    def _():
        @pl.when(tile >= 1)
        def _():
            _wait_runs(rows_ref, tile - 1, buf.at[1 - slot], xs_ref, sem.at[1 - slot], to_hbm=True)
        _wait_runs(rows_ref, tile, buf.at[slot], xs_ref, sem.at[slot], to_hbm=True)
        zero_fill(1, N_EXPERTS, zb_ref.shape[0], wait=True)


def _dispatch(run_len, run_row, tile_rows, zero_blocks, pos, h1, n_rows, tr, rs):
    t, d = h1.shape
    return pl.pallas_call(
        functools.partial(_dispatch_kernel, rs=rs),
        grid_spec=pltpu.PrefetchScalarGridSpec(
            num_scalar_prefetch=4, grid=(t // tr,),
            in_specs=[pl.BlockSpec((TOP_K, tr), lambda i, *_: (0, i)),
                      pl.BlockSpec((tr, d), lambda i, *_: (i, 0))],
            out_specs=pl.BlockSpec(memory_space=pl.ANY),
            scratch_shapes=[pltpu.VMEM((2, rs, d), BF16),
                            pltpu.VMEM((EXPERT_BLOCK, d), BF16),
                            pltpu.VMEM((tr, d), BF16),
                            pltpu.SemaphoreType.DMA((2,)), pltpu.SemaphoreType.DMA((2,))]),
        out_shape=jax.ShapeDtypeStruct((n_rows, d), BF16),
        compiler_params=_cparams(("arbitrary",)),
        name="dispatch",
    )(run_len, run_row, tile_rows, zero_blocks, pos, h1)


def _expert_kernel(be_ref, nu_ref, xs_ref, wg_ref, wu_ref, wd_ref, ys_ref, wgu_s, wd_s, *, ed):
    i = pl.program_id(0)
    live = i < nu_ref[0]
    new_expert = jnp.logical_or(i == 0, be_ref[i] != be_ref[jnp.maximum(i - 1, 0)])

    @pl.when(jnp.logical_and(live, new_expert))
    def _():
        wgu_s[:, :ed] = wg_ref[0].astype(BF16)
        wgu_s[:, ed:] = wu_ref[0].astype(BF16)
        wd_s[...] = wd_ref[0].astype(BF16)

    @pl.when(live)
    def _():
        gu = _dot(xs_ref[...], wgu_s[...])
        g = gu[:, :ed]
        act = (g * _sigmoid(g) * gu[:, ed:]).astype(BF16)
        ys_ref[...] = _dot(act, wd_s[...]).astype(BF16)


def _experts(block_e, n_used, xs, wg, wu, wd):
    n_rows, d = xs.shape
    n_blocks = n_rows // EXPERT_BLOCK
    ed = wg.shape[2]

    def xmap(i, be, nu):
        return (jnp.maximum(jnp.minimum(i, nu[0] - 1), 0), 0)

    def wmap(i, be, nu):
        return (be[i], 0, 0)

    return pl.pallas_call(
        functools.partial(_expert_kernel, ed=ed),
        grid_spec=pltpu.PrefetchScalarGridSpec(
            num_scalar_prefetch=2, grid=(n_blocks,),
            in_specs=[pl.BlockSpec((EXPERT_BLOCK, d), xmap),
                      pl.BlockSpec((1, d, ed), wmap),
                      pl.BlockSpec((1, d, ed), wmap),
                      pl.BlockSpec((1, ed, d), wmap)],
            out_specs=pl.BlockSpec((EXPERT_BLOCK, d), xmap),
            scratch_shapes=[pltpu.VMEM((d, 2 * ed), BF16), pltpu.VMEM((ed, d), BF16)]),
        out_shape=jax.ShapeDtypeStruct((n_rows, d), BF16),
        input_output_aliases={2: 0},
        compiler_params=_cparams(("arbitrary",)),
        name="experts",
    )(block_e, n_used, xs, wg, wu, wd)


def _combine_kernel(len_ref, row_ref, rows_ref, h1_ref, pos_ref, wk_ref, ys_ref, sgu_ref, sd_ref,
                    l2g_ref, l2b_ref, out_ref, ybuf, wmt, sem, *, rs, sdim):
    tile = pl.program_id(0)
    n_tiles = pl.num_programs(0)
    slot = tile % 2

    @pl.when(tile == 0)
    def _():
        ybuf[...] = jnp.zeros(ybuf.shape, ybuf.dtype)
        _start_runs(len_ref, row_ref, tile, ybuf.at[0], ys_ref, sem.at[0], to_hbm=False)

    @pl.when(tile + 1 < n_tiles)
    def _():
        _start_runs(len_ref, row_ref, tile + 1, ybuf.at[1 - slot], ys_ref, sem.at[1 - slot], to_hbm=False)

    _wait_runs(rows_ref, tile, ybuf.at[slot], ys_ref, sem.at[slot], to_hbm=False)

    h1 = h1_ref[...]
    gu = _dot(h1.astype(BF16), sgu_ref[...])
    g = gu[:, :sdim]
    act = (g * _sigmoid(g) * gu[:, sdim:]).astype(BF16)
    acc = DN_ALPHA * h1 + _dot(act, sd_ref[...])

    def weights_chunk(c):
        wmt[c * SORT_CHUNK:(c + 1) * SORT_CHUNK, :] = _assignment_rows(pos_ref, wk_ref, c, SORT_CHUNK)

    def unsort(lo, hi):
        return lax.dot_general(wmt[lo:hi, :], ybuf[slot, lo:hi, :], (((0,), (0,)), ((), ())),
                               preferred_element_type=F32)

    n_always = _chunks_always(h1.shape[0])
    for c in range(n_always):
        weights_chunk(c)
    out_ref[...] = acc + unsort(0, n_always * SORT_CHUNK)
    for c in range(n_always, rs // SORT_CHUNK):
        @pl.when(rows_ref[tile] > c * SORT_CHUNK)
        def _(c=c):
            weights_chunk(c)
            out_ref[...] += unsort(c * SORT_CHUNK, (c + 1) * SORT_CHUNK)
    out_ref[...] = _ln(out_ref[...], l2g_ref[...], l2b_ref[...])


def _combine(run_len, run_row, tile_rows, h1, pos, wk, ys, sgu, sd, l2g, l2b, tr, rs):
    t, d = h1.shape
    sdim = sd.shape[0]
    row = pl.BlockSpec((tr, d), lambda i, *_: (i, 0))
    krow = pl.BlockSpec((TOP_K, tr), lambda i, *_: (0, i))
    vec = pl.BlockSpec((1, d), lambda i, *_: (0, 0))
    return pl.pallas_call(
        functools.partial(_combine_kernel, rs=rs, sdim=sdim),
        grid_spec=pltpu.PrefetchScalarGridSpec(
            num_scalar_prefetch=3, grid=(t // tr,),
            in_specs=[row, krow, krow,
                      pl.BlockSpec(memory_space=pl.ANY),
                      pl.BlockSpec(sgu.shape, lambda i, *_: (0, 0)),
                      pl.BlockSpec(sd.shape, lambda i, *_: (0, 0)),
                      vec, vec],
            out_specs=row,
            scratch_shapes=[pltpu.VMEM((2, rs, d), BF16),
                            pltpu.VMEM((rs, tr), BF16),
                            pltpu.SemaphoreType.DMA((2,))]),
        out_shape=jax.ShapeDtypeStruct((t, d), F32),
        compiler_params=_cparams(("arbitrary",)),
        name="combine",
    )(run_len, run_row, tile_rows, h1, pos, wk, ys, sgu, sd, l2g, l2b)


def _tile(n, pref):
    while n % pref:
        pref //= 2
    return pref


def kernel(x, mem, ln_in_g, ln_in_b, w_in, gmlp_norm_g, gmlp_norm_b, gmlp_spatial_w, gmlp_spatial_b, conv_w, conv_b, lru_wa, lru_ba, lru_wi, lru_bi, lru_lambda, w_kv, w_br_gmlp, w_br_lru, w_br_xa, w_out, ln1_g, ln1_b, w_router, router_bias, exp_gate, exp_up, exp_down, sh_gate, sh_up, sh_down, ln2_g, ln2_b):
    b, s, d = x.shape
    t = b * s
    assert w_in.shape[0] == DEPTH and w_in.shape[2] == 8 * d and s % GMLP_CHUNK == 0
    r2 = lambda a: a.reshape(1, -1)
    x2 = x.reshape(t, d)
    lg, lb = r2(ln_in_g), r2(ln_in_b)
    hd = d // XA_HEADS

    wkv = w_kv[0]
    kt, vv = _kv(mem, wkv[:, :d].T.astype(BF16), wkv[:, d:].astype(BF16))

    tm = _tile(s, 512)
    assert tm % GMLP_CHUNK == 0
    wai = jnp.concatenate([lru_wa[0], lru_wi[0]], axis=-1).astype(BF16)
    yg, yl, yx, g0, g1, g2 = _inproj(
        x2, lg, lb, w_in[0].astype(BF16), r2(gmlp_norm_g[0]), r2(gmlp_norm_b[0]),
        conv_w[0], r2(conv_b[0]), wai, r2(lru_ba[0]), r2(lru_bi[0]), r2(lru_lambda[0]),
        gmlp_spatial_w[0], gmlp_spatial_b[0].T, kt, vv, b, tm, hd ** -0.5)

    tr = ROUTE_TILE
    assert tm % tr == 0
    n_tiles = t // tr
    rs = tr * TOP_K + N_EXPERTS * RUN_ALIGN
    rb = jnp.broadcast_to(router_bias[0].astype(F32)[:, None], (N_EXPERTS, 128))
    h1, pos, wk, run_len3, run_off3 = _merge(
        x2, lg, lb, yg, yl, yx, g0, g1, g2,
        w_br_gmlp[0].astype(BF16), w_br_lru[0].astype(BF16), w_br_xa[0].astype(BF16),
        w_out[0].astype(BF16), r2(ln1_g[0]), r2(ln1_b[0]), w_router[0].T, rb, tm)

    run_len = run_len3[:, :, 0]
    run_off = run_off3[:, :, 0]
    total = run_off[-1] + run_len[-1]
    seg_len = (total + EXPERT_BLOCK - 1) // EXPERT_BLOCK * EXPERT_BLOCK
    seg_end = jnp.cumsum(seg_len)
    seg_start = seg_end - seg_len
    run_row = seg_start[None, :] + run_off
    tile_rows = jnp.sum(run_len, axis=1).astype(jnp.int32)
    max_rows = t * TOP_K + n_tiles * N_EXPERTS * (RUN_ALIGN - 1)
    n_blocks = -(-max_rows // EXPERT_BLOCK) + N_EXPERTS
    block_start = jnp.arange(n_blocks, dtype=jnp.int32) * EXPERT_BLOCK
    block_e = jnp.minimum(jnp.sum(block_start[:, None] >= seg_end[None, :], axis=1),
                          N_EXPERTS - 1).astype(jnp.int32)
    n_used = (seg_end[-1:] // EXPERT_BLOCK).astype(jnp.int32)
    n_tail = n_blocks - (t * TOP_K) // EXPERT_BLOCK
    tail_blk = n_used[0] + jnp.arange(n_tail, dtype=jnp.int32)
    zero_blocks = jnp.concatenate([
        jnp.where(total % EXPERT_BLOCK != 0, seg_end // EXPERT_BLOCK - 1, -1),
        jnp.where(tail_blk < n_blocks, tail_blk, -1)]).astype(jnp.int32)

    run_len_flat = run_len.reshape(-1).astype(jnp.int32)
    run_row_flat = run_row.reshape(-1).astype(jnp.int32)
    xs = _dispatch(run_len_flat, run_row_flat, tile_rows, zero_blocks, pos, h1,
                   n_blocks * EXPERT_BLOCK, tr, rs)
    ys = _experts(block_e, n_used, xs, exp_gate[0], exp_up[0], exp_down[0])
    sgu = jnp.concatenate([sh_gate[0], sh_up[0]], axis=-1).astype(BF16)
    out = _combine(run_len_flat, run_row_flat, tile_rows, h1, pos, wk, ys, sgu, sh_down[0].astype(BF16),
                   r2(ln2_g[0]), r2(ln2_b[0]), tr, rs)
    return out.reshape(b, s, d)
```
